```python
import math
import jax
import jax.numpy as jnp
from jax import lax
import numpy as np

D_MODEL = 1024
BATCH = 8
SEQ = 2048
DEPTH = 2

CONV_A_GROUPS = 4
CONV_A_GROUP_DIM = 64
D_CONV_A = CONV_A_GROUPS * CONV_A_GROUP_DIM
CONV_A_WIDTH = 3
SSD_HEADS = 6
SSD_HEAD_DIM = 64
D_SSD = SSD_HEADS * SSD_HEAD_DIM
SSD_GROUPS = 2
SSD_STATE = 128
SSD_CONV_WIDTH = 4
SSD_CHUNK = 128
SSD_CONV_DIM = D_SSD + 2 * SSD_GROUPS * SSD_STATE
SSD_NORM_EPS = 1e-5
MLA_HEADS = 6
Q_LORA = 256
KV_LORA = 128
QK_NOPE = 64
QK_ROPE = 32
V_DIM = 64
D_MLA = MLA_HEADS * V_DIM
ROPE_BASE = 10000.0
Q_BLOCK = 128
D_MIX = D_CONV_A + D_SSD + D_MLA
NORM_EPS = 1e-6
POS_OFFSET_MAX = 1024
SPLIT_SIZES = (D_CONV_A, D_CONV_A, D_CONV_A, D_CONV_A,
               D_SSD, D_SSD, SSD_GROUPS * SSD_STATE, SSD_GROUPS * SSD_STATE, SSD_HEADS,
               Q_LORA, KV_LORA, QK_ROPE, D_MLA)
IN_COLS = sum(SPLIT_SIZES)

kernel_name = 'hybrid_conv_ssd_mla_parallel'


def rmsnorm(x, g, eps=NORM_EPS):
    xf = x.astype(jnp.float32)
    y = xf * lax.rsqrt(jnp.mean(xf * xf, axis=-1, keepdims=True) + eps)
    return (y * g.astype(jnp.float32)).astype(x.dtype)


def causal_depthwise_conv(u, w):
    k, c = w.shape
    return lax.conv_general_dilated(
        u, w[:, None, :].astype(u.dtype), window_strides=(1,), padding=[(k - 1, 0)],
        dimension_numbers=('NWC', 'WIO', 'NWC'), feature_group_count=c)


def apply_rope(t, cos, sin):
    tf = t.astype(jnp.float32)
    t1, t2 = jnp.split(tf, 2, axis=-1)
    return jnp.concatenate([t1 * cos - t2 * sin, t2 * cos + t1 * sin], axis=-1).astype(t.dtype)


def short_conv_branch(a_h, a_b, a_c, a_z, conv_w):
    return a_b * causal_depthwise_conv(a_c * a_h, conv_w) * jax.nn.silu(a_z)


def segsum_exp(a_cs):
    l = a_cs.shape[-1]
    diff = a_cs[..., :, None] - a_cs[..., None, :]
    mask = jnp.tril(jnp.ones((l, l), dtype=bool))
    return jnp.exp(jnp.where(mask, diff, -jnp.inf))


def ssd_chunked(xh, dt, a, bh, ch):
    b, s, h, p = xh.shape
    n = bh.shape[-1]
    nc = s // SSD_CHUNK
    la = (dt * a).reshape(b, nc, SSD_CHUNK, h).transpose(0, 3, 1, 2)
    xd = (xh * dt[..., None]).reshape(b, nc, SSD_CHUNK, h, p)
    bc = bh.reshape(b, nc, SSD_CHUNK, h, n)
    cc = ch.reshape(b, nc, SSD_CHUNK, h, n)
    a_cs = jnp.cumsum(la, axis=-1)
    scores = jnp.einsum('bclhn,bcshn->bhcls', cc, bc) * segsum_exp(a_cs)
    y_diag = jnp.einsum('bhcls,bcshp->bclhp', scores, xd)
    decay_states = jnp.exp(a_cs[..., -1:] - a_cs)
    states = jnp.einsum('bclhn,bhcl,bclhp->bchpn', bc, decay_states, xd)
    chunk_decay = jnp.exp(a_cs[..., -1])

    def step(carry, inp):
        st, dec = inp
        return carry * dec[..., None, None] + st, carry

    init = jnp.zeros((b, h, p, n), dtype=xd.dtype)
    _, prev = lax.scan(step, init, (jnp.moveaxis(states, 1, 0), jnp.moveaxis(chunk_decay, 2, 0)))
    prev = jnp.moveaxis(prev, 0, 1)
    y_off = jnp.einsum('bclhn,bchpn,bhcl->bclhp', cc, prev, jnp.exp(a_cs))
    return (y_diag + y_off).reshape(b, s, h, p)


def ssd_branch(s_z, s_x, s_b, s_c, s_dt, conv_w, conv_b, dt_bias, a_log, d_skip, norm_g):
    b, s, _ = s_x.shape
    f32 = jnp.float32
    xbc = jnp.concatenate([s_x, s_b, s_c], axis=-1)
    xbc = jax.nn.silu(causal_depthwise_conv(xbc, conv_w) + conv_b)
    xs, bs, cs = jnp.split(xbc, [D_SSD, D_SSD + SSD_GROUPS * SSD_STATE], axis=-1)
    rep = SSD_HEADS // SSD_GROUPS
    xh = xs.reshape(b, s, SSD_HEADS, SSD_HEAD_DIM).astype(f32)
    bh = jnp.repeat(bs.reshape(b, s, SSD_GROUPS, SSD_STATE), rep, axis=2).astype(f32)
    ch = jnp.repeat(cs.reshape(b, s, SSD_GROUPS, SSD_STATE), rep, axis=2).astype(f32)
    dt = jax.nn.softplus(s_dt.astype(f32) + dt_bias.astype(f32))
    a = -jnp.exp(a_log.astype(f32))
    y = ssd_chunked(xh, dt, a, bh, ch) + xh * d_skip.astype(f32)[:, None]
    y = y.reshape(b, s, D_SSD).astype(s_x.dtype)
    g = (y * jax.nn.silu(s_z)).reshape(b, s, SSD_GROUPS, D_SSD // SSD_GROUPS)
    g = rmsnorm(g, norm_g.reshape(SSD_GROUPS, D_SSD // SSD_GROUPS), SSD_NORM_EPS)
    return g.reshape(b, s, D_SSD)


def causal_block_attention(q_nope, q_rope, k_nope, k_rope, v):
    b, s, h, _ = q_nope.shape
    nb = s // Q_BLOCK
    scale = (QK_NOPE + QK_ROPE) ** -0.5
    kpos = jnp.arange(s)

    def to_blocks(t):
        return jnp.swapaxes(t.reshape(b, nb, Q_BLOCK, *t.shape[2:]), 0, 1)

    def one_block(args):
        qn, qr, start = args
        sc = (jnp.einsum('bqhd,bkhd->bhqk', qn, k_nope)
              + jnp.einsum('bqhr,bkr->bhqk', qr, k_rope)).astype(jnp.float32) * scale
        qpos = start + jnp.arange(Q_BLOCK)
        mask = kpos[None, :] <= qpos[:, None]
        pr = jax.nn.softmax(jnp.where(mask, sc, -jnp.inf), axis=-1).astype(v.dtype)
        return jnp.einsum('bhqk,bkhd->bqhd', pr, v)

    out = lax.map(one_block, (to_blocks(q_nope), to_blocks(q_rope), jnp.arange(nb) * Q_BLOCK))
    return jnp.swapaxes(out, 0, 1).reshape(b, s, h, v.shape[-1])


def mla_branch(c_qa, c_kv, c_kr, c_z, cos, sin, q_norm_g, w_qb, kv_norm_g, w_kvb):
    b, s, _ = c_qa.shape
    q = jnp.einsum('bsr,rc->bsc', rmsnorm(c_qa, q_norm_g), w_qb).reshape(b, s, MLA_HEADS, QK_NOPE + QK_ROPE)
    q_nope, q_rope = jnp.split(q, [QK_NOPE], axis=-1)
    q_rope = apply_rope(q_rope, cos[:, :, None, :], sin[:, :, None, :])
    kv = jnp.einsum('bsr,rc->bsc', rmsnorm(c_kv, kv_norm_g), w_kvb).reshape(b, s, MLA_HEADS, QK_NOPE + V_DIM)
    k_nope, v = jnp.split(kv, [QK_NOPE], axis=-1)
    k_rope = apply_rope(c_kr, cos, sin)
    o = causal_block_attention(q_nope, q_rope, k_nope, k_rope, v)
    return o.reshape(b, s, D_MLA) * jax.nn.silu(c_z)


def hybrid_layer(x, cos, sin, norm_g, w_in, conv_a_w, ssd_conv_w, ssd_conv_b, ssd_dt_bias,
                 ssd_a_log, ssd_d, ssd_norm_g, mla_q_norm_g, w_qb, mla_kv_norm_g, w_kvb, w_out):
    h = rmsnorm(x, norm_g)
    proj = jnp.einsum('bsd,dc->bsc', h, w_in)
    split_at = np.cumsum(SPLIT_SIZES)[:-1].tolist()
    (a_h, a_b, a_c, a_z, s_z, s_x, s_b, s_c, s_dt,
     c_qa, c_kv, c_kr, c_z) = jnp.split(proj, split_at, axis=-1)
    y_a = short_conv_branch(a_h, a_b, a_c, a_z, conv_a_w)
    y_b = ssd_branch(s_z, s_x, s_b, s_c, s_dt, ssd_conv_w, ssd_conv_b, ssd_dt_bias,
                     ssd_a_log, ssd_d, ssd_norm_g)
    y_c = mla_branch(c_qa, c_kv, c_kr, c_z, cos, sin, mla_q_norm_g, w_qb, mla_kv_norm_g, w_kvb)
    y = jnp.concatenate([y_a, y_b, y_c], axis=-1)
    return x + jnp.einsum('bsm,md->bsd', y, w_out)


def setup_inputs(seed: int = 0) -> dict:
    key = jax.random.key(seed)
    ks = jax.random.split(key, 20)
    f32 = jnp.float32
    nrm = jax.random.normal
    x = nrm(ks[0], (BATCH, SEQ, D_MODEL), f32)
    offs = jax.random.randint(ks[1], (BATCH, 1), 0, POS_OFFSET_MAX, dtype=jnp.int32)
    positions = (offs + jnp.arange(SEQ, dtype=jnp.int32)[None, :]).astype(jnp.int32)
    norm_g = 1.0 + 0.02 * nrm(ks[2], (DEPTH, D_MODEL), f32)
    w_in = nrm(ks[3], (DEPTH, D_MODEL, IN_COLS), f32) * D_MODEL ** -0.5
    conv_a_w = nrm(ks[4], (DEPTH, CONV_A_WIDTH, D_CONV_A), f32) * CONV_A_WIDTH ** -0.5
    ssd_conv_w = nrm(ks[5], (DEPTH, SSD_CONV_WIDTH, SSD_CONV_DIM), f32) * SSD_CONV_WIDTH ** -0.5
    ssd_conv_b = 0.01 * nrm(ks[6], (DEPTH, SSD_CONV_DIM), f32)
    u = jax.random.uniform(ks[7], (DEPTH, SSD_HEADS), f32)
    dt0 = jnp.exp(u * (math.log(0.1) - math.log(0.001)) + math.log(0.001))
    ssd_dt_bias = dt0 + jnp.log(-jnp.expm1(-dt0))
    ssd_a_log = jnp.log(jax.random.uniform(ks[8], (DEPTH, SSD_HEADS), f32, 1.0, 16.0))
    ssd_d = 1.0 + 0.1 * nrm(ks[9], (DEPTH, SSD_HEADS), f32)
    ssd_norm_g = 1.0 + 0.02 * nrm(ks[10], (DEPTH, D_SSD), f32)
    mla_q_norm_g = 1.0 + 0.02 * nrm(ks[11], (DEPTH, Q_LORA), f32)
    w_qb = nrm(ks[12], (DEPTH, Q_LORA, MLA_HEADS * (QK_NOPE + QK_ROPE)), f32) * Q_LORA ** -0.5
    mla_kv_norm_g = 1.0 + 0.02 * nrm(ks[13], (DEPTH, KV_LORA), f32)
    w_kvb = nrm(ks[14], (DEPTH, KV_LORA, MLA_HEADS * (QK_NOPE + V_DIM)), f32) * KV_LORA ** -0.5
    w_out = nrm(ks[15], (DEPTH, D_MIX, D_MODEL), f32) * D_MIX ** -0.5
    final_norm_g = 1.0 + 0.02 * nrm(ks[16], (D_MODEL,), f32)
    return {'x': x, 'positions': positions, 'norm_g': norm_g, 'w_in': w_in, 'conv_a_w': conv_a_w,
            'ssd_conv_w': ssd_conv_w, 'ssd_conv_b': ssd_conv_b, 'ssd_dt_bias': ssd_dt_bias,
            'ssd_a_log': ssd_a_log, 'ssd_d': ssd_d, 'ssd_norm_g': ssd_norm_g,
            'mla_q_norm_g': mla_q_norm_g, 'w_qb': w_qb, 'mla_kv_norm_g': mla_kv_norm_g,
            'w_kvb': w_kvb, 'w_out': w_out, 'final_norm_g': final_norm_g}


def reference(x, positions, norm_g, w_in, conv_a_w, ssd_conv_w, ssd_conv_b, ssd_dt_bias,
              ssd_a_log, ssd_d, ssd_norm_g, mla_q_norm_g, w_qb, mla_kv_norm_g, w_kvb, w_out,
              final_norm_g):
    inv_freq = ROPE_BASE ** (-jnp.arange(0, QK_ROPE, 2, dtype=jnp.float32) / QK_ROPE)
    ang = positions.astype(jnp.float32)[..., None] * inv_freq
    cos, sin = jnp.cos(ang), jnp.sin(ang)
    for l in range(DEPTH):
        x = hybrid_layer(x, cos, sin, norm_g[l], w_in[l], conv_a_w[l], ssd_conv_w[l], ssd_conv_b[l],
                         ssd_dt_bias[l], ssd_a_log[l], ssd_d[l], ssd_norm_g[l], mla_q_norm_g[l],
                         w_qb[l], mla_kv_norm_g[l], w_kvb[l], w_out[l])
    return rmsnorm(x, final_norm_g)
```

```python
import functools

import jax
import jax.numpy as jnp
from jax import lax
from jax.experimental import pallas as pl
from jax.experimental.pallas import tpu as pltpu

F32 = jnp.float32
BF16 = jnp.bfloat16

D_MODEL = 1024
D_CONV_A = 256
CONV_A_WIDTH = 3
SSD_HEADS = 6
SSD_HEAD_DIM = 64
D_SSD = SSD_HEADS * SSD_HEAD_DIM
SSD_GROUPS = 2
SSD_STATE = 128
SSD_CONV_WIDTH = 4
SSD_BC = SSD_GROUPS * SSD_STATE
SSD_CONV_DIM = D_SSD + 2 * SSD_BC
SSD_NORM_EPS = 1e-5
MLA_HEADS = 6
Q_LORA = 256
KV_LORA = 128
QK_NOPE = 64
QK_ROPE = 32
V_DIM = 64
D_MLA = MLA_HEADS * V_DIM
ROPE_BASE = 10000.0
D_MIX = D_CONV_A + D_SSD + D_MLA
NORM_EPS = 1e-6

LANES = 128
SUBLANES = 8
V7X_VMEM_BYTES = 64 * 1024 * 1024

SEQ_TILE = 256
SSD_CHUNK = 128
HEAD_SLAB = LANES
ROPE_HALF = QK_ROPE // 2
ROPE_ROW0 = QK_NOPE
ONES_ROW = V_DIM
HALO = SUBLANES

OFF_A = 0
OFF_SZ = OFF_A + 4 * D_CONV_A
OFF_XBC = OFF_SZ + D_SSD
OFF_DT = OFF_XBC + SSD_CONV_DIM
OFF_QA = OFF_DT + D_SSD
OFF_KV = OFF_QA + Q_LORA
OFF_CZ = OFF_KV + KV_LORA
N_PROJ = OFF_CZ + D_MLA

ROW_NORM_G = 0
ROW_CONV_A = 1
ROW_SSD_CONV = ROW_CONV_A + CONV_A_WIDTH
ROW_SSD_CONV_B = ROW_SSD_CONV + SSD_CONV_WIDTH
ROW_DT_BIAS = ROW_SSD_CONV_B + 1
ROW_A_LOG = ROW_DT_BIAS + 1
ROW_D_SKIP = ROW_A_LOG + 1
ROW_SSD_NORM_G = ROW_D_SKIP + 1
ROW_Q_NORM_G = ROW_SSD_NORM_G + 1
ROW_KV_NORM_G = ROW_Q_NORM_G + 1
ROW_FINAL_G = ROW_KV_NORM_G + 1
N_VEC_ROWS = 16


def _silu(v):
    return v * (1.0 / (1.0 + jnp.exp(-v)))


def _softplus(v):
    return jnp.maximum(v, 0.0) + jnp.log1p(jnp.exp(-jnp.abs(v)))


def _split3(v):
    hi = v.astype(BF16)
    r1 = v - hi.astype(F32)
    mid = r1.astype(BF16)
    lo = (r1 - mid.astype(F32)).astype(BF16)
    return hi, mid, lo


def _dot(a, b):
    return jnp.dot(a, b, preferred_element_type=F32)


def _dot_nt(a, b):
    return lax.dot_general(a, b, (((1,), (1,)), ((), ())), preferred_element_type=F32)


def _layer_kernel(pos_ref, x_ref, invf_ref, vec_ref, w_in_ref, w_krt_ref, wqbt_ref, wk_ref,
                  wvt_ref, w_out_ref, o_ref,
                  proj_ref, ua_ref, xbc_ref, state_ref, kc_ref, vt_ref, qt_ref, acc_ref,
                  m_ref, yt_ref, ycat_ref, *, final):
    j = pl.program_id(1)
    tile = x_ref.shape[0]

    def vec(row, width):
        return vec_ref[row:row + 1, 0:width]

    x = x_ref[...]
    h = x * lax.rsqrt(jnp.mean(x * x, axis=-1, keepdims=True) + NORM_EPS) * vec(ROW_NORM_G, D_MODEL)
    hb = h.astype(BF16)
    proj_ref[...] = _dot(hb, w_in_ref[...])
    ckr_t = _dot_nt(w_krt_ref[...], hb)

    @pl.when(j == 0)
    def _():
        ua_ref[0:HALO, :] = jnp.zeros((HALO, D_CONV_A), F32)
        xbc_ref[0:HALO, :] = jnp.zeros((HALO, SSD_CONV_DIM), F32)
        state_ref[...] = jnp.zeros(state_ref.shape, F32)

    a_h = proj_ref[:, OFF_A:OFF_A + D_CONV_A]
    a_b = proj_ref[:, OFF_A + D_CONV_A:OFF_A + 2 * D_CONV_A]
    a_c = proj_ref[:, OFF_A + 2 * D_CONV_A:OFF_A + 3 * D_CONV_A]
    a_z = proj_ref[:, OFF_A + 3 * D_CONV_A:OFF_A + 4 * D_CONV_A]
    ua_ref[HALO:HALO + tile, :] = a_c * a_h
    conv_a = vec(ROW_CONV_A + CONV_A_WIDTH - 1, D_CONV_A) * ua_ref[HALO:HALO + tile, :]
    for k in range(1, CONV_A_WIDTH):
        conv_a += vec(ROW_CONV_A + CONV_A_WIDTH - 1 - k, D_CONV_A) * ua_ref[HALO - k:HALO - k + tile, :]
    ua_ref[0:HALO, :] = ua_ref[tile:tile + HALO, :]
    ycat_ref[:, 0:D_CONV_A] = (a_b * conv_a * _silu(a_z)).astype(BF16)

    xbc_ref[HALO:HALO + tile, :] = proj_ref[:, OFF_XBC:OFF_XBC + SSD_CONV_DIM]
    conv_b = vec(ROW_SSD_CONV + SSD_CONV_WIDTH - 1, SSD_CONV_DIM) * xbc_ref[HALO:HALO + tile, :]
    for k in range(1, SSD_CONV_WIDTH):
        conv_b += (vec(ROW_SSD_CONV + SSD_CONV_WIDTH - 1 - k, SSD_CONV_DIM)
                   * xbc_ref[HALO - k:HALO - k + tile, :])
    xbc_ref[0:HALO, :] = xbc_ref[tile:tile + HALO, :]
    xbc = _silu(conv_b + vec(ROW_SSD_CONV_B, SSD_CONV_DIM))
    dt = _softplus(proj_ref[:, OFF_DT:OFF_DT + D_SSD] + vec(ROW_DT_BIAS, D_SSD))
    a_neg = -jnp.exp(vec(ROW_A_LOG, D_SSD))
    s_z = proj_ref[:, OFF_SZ:OFF_SZ + D_SSD]

    L = SSD_CHUNK
    row_i = lax.broadcasted_iota(jnp.int32, (L, L), 0)
    col_i = lax.broadcasted_iota(jnp.int32, (L, L), 1)
    causal = row_i >= col_i
    tril = causal.astype(BF16)
    lane_lo = lax.broadcasted_iota(jnp.int32, (L, LANES), 1) < SSD_HEAD_DIM
    half = D_SSD // SSD_GROUPS
    grp0 = lax.broadcasted_iota(jnp.int32, (1, D_SSD), 1) < half

    y_chunks = []
    for c in range(tile // L):
        rows = slice(c * L, (c + 1) * L)
        xs = xbc[rows, 0:D_SSD]
        bs = xbc[rows, D_SSD:D_SSD + SSD_BC]
        cs = xbc[rows, D_SSD + SSD_BC:SSD_CONV_DIM]
        dt_c = dt[rows, :]
        la = dt_c * a_neg
        p_hi, p_mid, p_lo = _split3(la)
        cum = _dot(tril, p_hi) + _dot(tril, p_mid) + _dot(tril, p_lo)
        tot = cum[L - 1:L, :]
        xd = xs * dt_c
        xd_b = xd.astype(BF16)
        bs_b = bs.astype(BF16)
        cs_b = cs.astype(BF16)
        state_b = state_ref[...].astype(BF16)

        cb = [_dot_nt(cs_b[:, g * SSD_STATE:(g + 1) * SSD_STATE],
                      bs_b[:, g * SSD_STATE:(g + 1) * SSD_STATE]) for g in range(SSD_GROUPS)]
        y_diag_slabs = []
        for s in range(D_SSD // LANES):
            cum_slab = cum[:, s * LANES:(s + 1) * LANES]
            cum_rolled = pltpu.roll(cum_slab, SSD_HEAD_DIM, 1)
            cum_t = cum_slab.T
            pair = []
            for e in range(2):
                hd = 2 * s + e
                col = jnp.where(lane_lo, cum_slab, cum_rolled) if e == 0 else jnp.where(lane_lo, cum_rolled, cum_slab)
                row = cum_t[e * SSD_HEAD_DIM:e * SSD_HEAD_DIM + 1, :]
                decay = jnp.exp(jnp.where(causal, col - row, -jnp.inf))
                scores = (cb[hd // (SSD_HEADS // SSD_GROUPS)] * decay).astype(BF16)
                pair.append(_dot(scores, xd_b[:, s * LANES:(s + 1) * LANES]))
            y_diag_slabs.append(jnp.where(lane_lo, pair[0], pair[1]))
        y_diag = jnp.concatenate(y_diag_slabs, axis=1)

        y_off_g = [_dot(cs_b[:, g * SSD_STATE:(g + 1) * SSD_STATE], state_b) for g in range(SSD_GROUPS)]
        y_off = jnp.where(grp0, y_off_g[0], y_off_g[1]) * jnp.exp(cum)

        xdd = (xd * jnp.exp(tot - cum)).astype(BF16)
        add_g = [_dot(bs[:, g * SSD_STATE:(g + 1) * SSD_STATE].T.astype(BF16), xdd) for g in range(SSD_GROUPS)]
        state_ref[...] = state_ref[...] * jnp.exp(tot) + jnp.where(grp0, add_g[0], add_g[1])

        y_chunks.append(y_diag + y_off + xs * vec(ROW_D_SKIP, D_SSD))
    y_ssd = jnp.concatenate(y_chunks, axis=0)
    gated = y_ssd * _silu(s_z)
    g2 = gated * gated
    ss_all = jnp.sum(g2, axis=-1, keepdims=True)
    ss0 = jnp.sum(jnp.where(grp0, g2, 0.0), axis=-1, keepdims=True)
    inv0 = lax.rsqrt(ss0 * (1.0 / half) + SSD_NORM_EPS)
    inv1 = lax.rsqrt((ss_all - ss0) * (1.0 / half) + SSD_NORM_EPS)
    y_b = gated * jnp.where(grp0, inv0, inv1) * vec(ROW_SSD_NORM_G, D_SSD)
    ycat_ref[:, D_CONV_A:D_CONV_A + D_SSD] = y_b.astype(BF16)

    scale = (QK_NOPE + QK_ROPE) ** -0.5
    ang = pos_ref[...].astype(F32) * invf_ref[...]
    cos_t = jnp.cos(ang)
    sin_t = jnp.sin(ang)

    def rope_rows(t1, t2):
        return t1 * cos_t - t2 * sin_t, t2 * cos_t + t1 * sin_t

    c_qa = proj_ref[:, OFF_QA:OFF_QA + Q_LORA]
    qan = (c_qa * lax.rsqrt(jnp.mean(c_qa * c_qa, axis=-1, keepdims=True) + NORM_EPS)
           * vec(ROW_Q_NORM_G, Q_LORA)).astype(BF16)
    q_t = _dot_nt(wqbt_ref[...], qan)
    r0 = ROPE_ROW0
    for hd in range(MLA_HEADS):
        base = hd * HEAD_SLAB
        q1, q2 = rope_rows(q_t[base + r0:base + r0 + ROPE_HALF, :],
                           q_t[base + r0 + ROPE_HALF:base + r0 + QK_ROPE, :])
        slab = jnp.concatenate([q_t[base:base + r0, :], q1, q2,
                                q_t[base + r0 + QK_ROPE:base + HEAD_SLAB, :]], axis=0)
        qt_ref[base:base + HEAD_SLAB, :] = (slab * scale).astype(BF16)

    c_kv = proj_ref[:, OFF_KV:OFF_KV + KV_LORA]
    kvn = (c_kv * lax.rsqrt(jnp.mean(c_kv * c_kv, axis=-1, keepdims=True) + NORM_EPS)
           * vec(ROW_KV_NORM_G, KV_LORA)).astype(BF16)
    k1, k2 = rope_rows(ckr_t[r0:r0 + ROPE_HALF, :], ckr_t[r0 + ROPE_HALF:r0 + QK_ROPE, :])
    kr_nat = jnp.concatenate([ckr_t[0:r0, :], k1, k2, ckr_t[r0 + QK_ROPE:HEAD_SLAB, :]], axis=0).T
    k_nat = _dot(kvn, wk_ref[...])
    v_t = _dot_nt(wvt_ref[...], kvn)
    ones_row = lax.broadcasted_iota(jnp.int32, (HEAD_SLAB, tile), 0) == ONES_ROW
    for hd in range(MLA_HEADS):
        base = hd * HEAD_SLAB
        kc_ref[j, :, base:base + HEAD_SLAB] = (k_nat[:, base:base + HEAD_SLAB] + kr_nat).astype(BF16)
        vt_ref[j, base:base + HEAD_SLAB, :] = jnp.where(ones_row, 1.0, v_t[base:base + HEAD_SLAB, :]).astype(BF16)

    m_ref[...] = jnp.full(m_ref.shape, -jnp.inf, F32)
    acc_ref[...] = jnp.zeros(acc_ref.shape, F32)
    key_i = lax.broadcasted_iota(jnp.int32, (tile, tile), 0)
    qry_i = lax.broadcasted_iota(jnp.int32, (tile, tile), 1)

    def attend(i, masked):
        for hd in range(MLA_HEADS):
            base = hd * HEAD_SLAB
            s_t = _dot(kc_ref[i, :, base:base + HEAD_SLAB], qt_ref[base:base + HEAD_SLAB, :])
            if masked:
                s_t = jnp.where(key_i <= qry_i, s_t, -jnp.inf)
            m_old = m_ref[hd]
            m_new = jnp.maximum(m_old, jnp.max(s_t, axis=0, keepdims=True))
            p = jnp.exp(s_t - m_new).astype(BF16)
            alpha = jnp.exp(m_old - m_new)
            acc_ref[base:base + HEAD_SLAB, :] = (alpha * acc_ref[base:base + HEAD_SLAB, :]
                                                 + _dot(vt_ref[i, base:base + HEAD_SLAB, :], p))
            m_ref[hd] = m_new

    def body(i, carry):
        attend(i, False)
        return carry

    lax.fori_loop(0, j, body, 0)
    attend(j, True)

    for hd in range(MLA_HEADS):
        base = hd * HEAD_SLAB
        yt_ref[hd * V_DIM:(hd + 1) * V_DIM, :] = (acc_ref[base:base + V_DIM, :]
                                                  / acc_ref[base + ONES_ROW:base + ONES_ROW + 1, :])
    c_z = proj_ref[:, OFF_CZ:OFF_CZ + D_MLA]
    ycat_ref[:, D_CONV_A + D_SSD:D_MIX] = (yt_ref[...].T * _silu(c_z)).astype(BF16)

    out = x + _dot(ycat_ref[...], w_out_ref[...])
    if final:
        out = (out * lax.rsqrt(jnp.mean(out * out, axis=-1, keepdims=True) + NORM_EPS)
               * vec(ROW_FINAL_G, D_MODEL))
    o_ref[...] = out


def _prep_layer(norm_g, w_in, conv_a_w, ssd_conv_w, ssd_conv_b, ssd_dt_bias, ssd_a_log, ssd_d,
                ssd_norm_g, mla_q_norm_g, w_qb, mla_kv_norm_g, w_kvb, w_out, final_norm_g):
    o_sdt = 4 * D_CONV_A + 2 * D_SSD + 2 * SSD_BC
    o_qa = o_sdt + SSD_HEADS
    o_kv = o_qa + Q_LORA
    o_kr = o_kv + KV_LORA
    o_cz = o_kr + QK_ROPE
    rep = lambda v: jnp.repeat(v, SSD_HEAD_DIM, axis=-1)
    w_nat = jnp.concatenate([
        w_in[:, 0:o_sdt], rep(w_in[:, o_sdt:o_qa]), w_in[:, o_qa:o_kr], w_in[:, o_cz:o_cz + D_MLA]],
        axis=1).astype(BF16)
    w_krt = jnp.pad(w_in[:, o_kr:o_cz].T, ((ROPE_ROW0, HEAD_SLAB - ROPE_ROW0 - QK_ROPE), (0, 0))).astype(BF16)
    wqb = w_qb.reshape(Q_LORA, MLA_HEADS, QK_NOPE + QK_ROPE)
    wqbt = jnp.pad(wqb, ((0, 0), (0, 0), (0, HEAD_SLAB - QK_NOPE - QK_ROPE))).reshape(
        Q_LORA, MLA_HEADS * HEAD_SLAB).T.astype(BF16)
    wkv = w_kvb.reshape(KV_LORA, MLA_HEADS, QK_NOPE + V_DIM)
    wk = jnp.pad(wkv[:, :, :QK_NOPE], ((0, 0), (0, 0), (0, HEAD_SLAB - QK_NOPE))).reshape(
        KV_LORA, MLA_HEADS * HEAD_SLAB).astype(BF16)
    wvt = jnp.pad(wkv[:, :, QK_NOPE:], ((0, 0), (0, 0), (0, HEAD_SLAB - V_DIM))).reshape(
        KV_LORA, MLA_HEADS * HEAD_SLAB).T.astype(BF16)

    def row(v):
        return jnp.pad(v.astype(F32), (0, D_MODEL - v.shape[0]))[None, :]

    rows = [row(norm_g)]
    rows += [row(conv_a_w[k]) for k in range(CONV_A_WIDTH)]
    rows += [row(ssd_conv_w[k]) for k in range(SSD_CONV_WIDTH)]
    rows += [row(ssd_conv_b), row(rep(ssd_dt_bias)), row(rep(ssd_a_log)), row(rep(ssd_d)),
             row(ssd_norm_g), row(mla_q_norm_g), row(mla_kv_norm_g), row(final_norm_g)]
    rows += [jnp.zeros((N_VEC_ROWS - len(rows), D_MODEL), F32)]
    vecs = jnp.concatenate(rows, axis=0)
    return vecs, w_nat, w_krt, wqbt, wk, wvt, w_out.astype(BF16)


def _vmem_limit(tile, n_tiles):
    f32, bf16 = 4, 2
    blocks = 2 * (2 * tile * D_MODEL * f32)
    weights = 2 * bf16 * (D_MODEL * N_PROJ + HEAD_SLAB * D_MODEL + 2 * Q_LORA * MLA_HEADS * HEAD_SLAB
                          + 2 * KV_LORA * MLA_HEADS * HEAD_SLAB + D_MIX * D_MODEL)
    scratch = (tile * N_PROJ * f32 + (tile + HALO) * (D_CONV_A + SSD_CONV_DIM) * f32
               + SSD_STATE * D_SSD * f32 + 2 * n_tiles * tile * MLA_HEADS * HEAD_SLAB * bf16
               + MLA_HEADS * HEAD_SLAB * tile * (bf16 + f32) + D_MLA * tile * f32 + tile * D_MIX * bf16)
    temporaries = 4 * tile * N_PROJ * f32
    return min(blocks + weights + scratch + temporaries, V7X_VMEM_BYTES - 8 * 1024 * 1024)


def _layer_call(x, pos3, invf, params, *, final):
    batch, seq, d_model = x.shape
    tile = SEQ_TILE
    n_tiles = seq // tile
    vecs, w_nat, w_krt, wqbt, wk, wvt, w_out = params
    const = lambda shape: pl.BlockSpec(shape, lambda b, j: (0,) * len(shape))
    return pl.pallas_call(
        functools.partial(_layer_kernel, final=final),
        out_shape=jax.ShapeDtypeStruct(x.shape, F32),
        grid=(batch, n_tiles),
        in_specs=[
            pl.BlockSpec((None, 1, tile), lambda b, j: (b, 0, j)),
            pl.BlockSpec((None, tile, d_model), lambda b, j: (b, j, 0)),
            const(invf.shape), const(vecs.shape), const(w_nat.shape), const(w_krt.shape),
            const(wqbt.shape), const(wk.shape), const(wvt.shape), const(w_out.shape),
        ],
        out_specs=pl.BlockSpec((None, tile, d_model), lambda b, j: (b, j, 0)),
        scratch_shapes=[
            pltpu.VMEM((tile, N_PROJ), F32),
            pltpu.VMEM((tile + HALO, D_CONV_A), F32),
            pltpu.VMEM((tile + HALO, SSD_CONV_DIM), F32),
            pltpu.VMEM((SSD_STATE, D_SSD), F32),
            pltpu.VMEM((n_tiles, tile, MLA_HEADS * HEAD_SLAB), BF16),
            pltpu.VMEM((n_tiles, MLA_HEADS * HEAD_SLAB, tile), BF16),
            pltpu.VMEM((MLA_HEADS * HEAD_SLAB, tile), BF16),
            pltpu.VMEM((MLA_HEADS * HEAD_SLAB, tile), F32),
            pltpu.VMEM((MLA_HEADS, 1, tile), F32),
            pltpu.VMEM((D_MLA, tile), F32),
            pltpu.VMEM((tile, D_MIX), BF16),
        ],
        compiler_params=pltpu.CompilerParams(
            dimension_semantics=("arbitrary", "arbitrary"),
            vmem_limit_bytes=_vmem_limit(tile, n_tiles)),
        name="hybrid_layer_final" if final else "hybrid_layer",
    )(pos3, x, invf, vecs, w_nat, w_krt, wqbt, wk, wvt, w_out)


def kernel(x, positions, norm_g, w_in, conv_a_w, ssd_conv_w, ssd_conv_b, ssd_dt_bias, ssd_a_log, ssd_d,
           ssd_norm_g, mla_q_norm_g, w_qb, mla_kv_norm_g, w_kvb, w_out, final_norm_g):
    batch, seq, _ = x.shape
    depth = norm_g.shape[0]
    assert seq % SEQ_TILE == 0 and SEQ_TILE % SSD_CHUNK == 0
    inv_freq = ROPE_BASE ** (-jnp.arange(0, QK_ROPE, 2, dtype=F32) / QK_ROPE)
    invf = jnp.broadcast_to(inv_freq[:, None], (ROPE_HALF, SEQ_TILE))
    pos3 = positions.reshape(batch, 1, seq)
    for l in range(depth):
        params = _prep_layer(norm_g[l], w_in[l], conv_a_w[l], ssd_conv_w[l], ssd_conv_b[l], ssd_dt_bias[l],
                             ssd_a_log[l], ssd_d[l], ssd_norm_g[l], mla_q_norm_g[l], w_qb[l],
                             mla_kv_norm_g[l], w_kvb[l], w_out[l], final_norm_g)
        x = _layer_call(x, pos3, invf, params, final=(l == depth - 1))
    return x
```

```python
import functools

import jax
import jax.numpy as jnp
from jax import lax
from jax.experimental import pallas as pl
from jax.experimental.pallas import tpu as pltpu

F32 = jnp.float32
BF16 = jnp.bfloat16

D_MODEL = 1024
D_CONV_A = 256
CONV_A_WIDTH = 3
SSD_HEADS = 6
SSD_HEAD_DIM = 64
D_SSD = SSD_HEADS * SSD_HEAD_DIM
SSD_GROUPS = 2
SSD_STATE = 128
SSD_CONV_WIDTH = 4
SSD_BC = SSD_GROUPS * SSD_STATE
SSD_CONV_DIM = D_SSD + 2 * SSD_BC
SSD_NORM_EPS = 1e-5
MLA_HEADS = 6
Q_LORA = 256
KV_LORA = 128
QK_NOPE = 64
QK_ROPE = 32
V_DIM = 64
D_MLA = MLA_HEADS * V_DIM
ROPE_BASE = 10000.0
D_MIX = D_CONV_A + D_SSD + D_MLA
NORM_EPS = 1e-6

LANES = 128
SUBLANES = 8
V7X_VMEM_BYTES = 64 * 1024 * 1024

SEQ_TILE = 512
SSD_CHUNK = 128
HEAD_SLAB = LANES
ROPE_HALF = QK_ROPE // 2
ROPE_ROW0 = QK_NOPE
ONES_ROW = V_DIM
V_SLAB = 80
KEY_BLOCK = 128
SCORE_LOOKAHEAD = 4
PROJ_LOOKAHEAD = 1
LOG2E = 1.4426950408889634
HALO = SUBLANES

OFF_A = 0
OFF_SZ = OFF_A + 4 * D_CONV_A
OFF_XBC = OFF_SZ + D_SSD
OFF_DT = OFF_XBC + SSD_CONV_DIM
OFF_QA = OFF_DT + D_SSD
OFF_KV = OFF_QA + Q_LORA
OFF_CZ = OFF_KV + KV_LORA
N_PROJ = OFF_CZ + D_MLA

ROW_NORM_G = 0
ROW_CONV_A = 1
ROW_SSD_CONV = ROW_CONV_A + CONV_A_WIDTH
ROW_SSD_CONV_B = ROW_SSD_CONV + SSD_CONV_WIDTH
ROW_DT_BIAS = ROW_SSD_CONV_B + 1
ROW_A_LOG = ROW_DT_BIAS + 1
ROW_D_SKIP = ROW_A_LOG + 1
ROW_SSD_NORM_G = ROW_D_SKIP + 1
ROW_Q_NORM_G = ROW_SSD_NORM_G + 1
ROW_KV_NORM_G = ROW_Q_NORM_G + 1
ROW_FINAL_G = ROW_KV_NORM_G + 1
N_VEC_ROWS = 16


def _silu(v):
    return v * (1.0 / (1.0 + jnp.exp(-v)))


def _softplus(v):
    return jnp.maximum(v, 0.0) + jnp.log1p(jnp.exp(-jnp.abs(v)))


def _split3(v):
    hi = v.astype(BF16)
    r1 = v - hi.astype(F32)
    mid = r1.astype(BF16)
    lo = (r1 - mid.astype(F32)).astype(BF16)
    return hi, mid, lo


def _dot(a, b):
    return jnp.dot(a, b, preferred_element_type=F32)


def _dot_nt(a, b):
    return lax.dot_general(a, b, (((1,), (1,)), ((), ())), preferred_element_type=F32)


def _layer_kernel(pos_ref, x_ref, invf_ref, vec_ref, w_in_ref, w_krt_ref, wqbt_ref, wk_ref,
                  wvt_ref, w_out_ref, o_ref,
                  proj_ref, ckrt_ref, ua_ref, xbc_ref, state_ref, kc_ref, vt_ref, qt_ref, acc_ref,
                  m_ref, yt_ref, ycat_ref, *, final):
    j = pl.program_id(1)
    tile = x_ref.shape[0]

    def vec(row, width):
        return vec_ref[row:row + 1, 0:width]

    L = SSD_CHUNK
    n_chunks = tile // L

    @pl.when(j == 0)
    def _():
        ua_ref[0:HALO, :] = jnp.zeros((HALO, D_CONV_A), F32)
        xbc_ref[0:HALO, :] = jnp.zeros((HALO, SSD_CONV_DIM), F32)
        state_ref[...] = jnp.zeros(state_ref.shape, F32)

    row_i = lax.broadcasted_iota(jnp.int32, (L, L), 0)
    col_i = lax.broadcasted_iota(jnp.int32, (L, L), 1)
    causal = row_i >= col_i
    tril = causal.astype(BF16)
    lane_lo = lax.broadcasted_iota(jnp.int32, (L, LANES), 1) < SSD_HEAD_DIM
    half = D_SSD // SSD_GROUPS
    grp0 = lax.broadcasted_iota(jnp.int32, (1, D_SSD), 1) < half
    a_neg = -jnp.exp(vec(ROW_A_LOG, D_SSD))

    def project(c):
        rows = slice(c * L, (c + 1) * L)
        x = x_ref[rows, :]
        h = x * lax.rsqrt(jnp.mean(x * x, axis=-1, keepdims=True) + NORM_EPS) * vec(ROW_NORM_G, D_MODEL)
        hb = h.astype(BF16)
        proj_ref[rows, :] = _dot(hb, w_in_ref[...])
        ckrt_ref[:, rows] = _dot_nt(w_krt_ref[...], hb)

    def mix(c):
        rows = slice(c * L, (c + 1) * L)
        lo = HALO + c * L

        def pcol(off, width):
            return proj_ref[rows, off:off + width]

        ua_ref[lo:lo + L, :] = pcol(OFF_A + 2 * D_CONV_A, D_CONV_A) * pcol(OFF_A, D_CONV_A)
        conv_a = vec(ROW_CONV_A + CONV_A_WIDTH - 1, D_CONV_A) * ua_ref[lo:lo + L, :]
        for k in range(1, CONV_A_WIDTH):
            conv_a += vec(ROW_CONV_A + CONV_A_WIDTH - 1 - k, D_CONV_A) * ua_ref[lo - k:lo - k + L, :]
        ycat_ref[rows, 0:D_CONV_A] = (pcol(OFF_A + D_CONV_A, D_CONV_A) * conv_a
                                      * _silu(pcol(OFF_A + 3 * D_CONV_A, D_CONV_A))).astype(BF16)

        xbc_ref[lo:lo + L, :] = pcol(OFF_XBC, SSD_CONV_DIM)
        conv_b = vec(ROW_SSD_CONV + SSD_CONV_WIDTH - 1, SSD_CONV_DIM) * xbc_ref[lo:lo + L, :]
        for k in range(1, SSD_CONV_WIDTH):
            conv_b += (vec(ROW_SSD_CONV + SSD_CONV_WIDTH - 1 - k, SSD_CONV_DIM)
                       * xbc_ref[lo - k:lo - k + L, :])
        xbc = _silu(conv_b + vec(ROW_SSD_CONV_B, SSD_CONV_DIM))
        xs = xbc[:, 0:D_SSD]
        bs = xbc[:, D_SSD:D_SSD + SSD_BC]
        cs = xbc[:, D_SSD + SSD_BC:SSD_CONV_DIM]
        dt_c = _softplus(pcol(OFF_DT, D_SSD) + vec(ROW_DT_BIAS, D_SSD))
        la = dt_c * a_neg
        p_hi, p_mid, p_lo = _split3(la)
        cum = _dot(tril, p_hi) + _dot(tril, p_mid) + _dot(tril, p_lo)
        tot = cum[L - 1:L, :]
        xd = xs * dt_c
        xd_b = xd.astype(BF16)
        bs_b = bs.astype(BF16)
        cs_b = cs.astype(BF16)
        state_b = state_ref[...].astype(BF16)

        cb = [_dot_nt(cs_b[:, g * SSD_STATE:(g + 1) * SSD_STATE],
                      bs_b[:, g * SSD_STATE:(g + 1) * SSD_STATE]) for g in range(SSD_GROUPS)]
        y_diag_slabs = []
        for s in range(D_SSD // LANES):
            cum_slab = cum[:, s * LANES:(s + 1) * LANES]
            cum_rolled = pltpu.roll(cum_slab, SSD_HEAD_DIM, 1)
            cum_t = cum_slab.T
            pair = []
            for e in range(2):
                hd = 2 * s + e
                col = jnp.where(lane_lo, cum_slab, cum_rolled) if e == 0 else jnp.where(lane_lo, cum_rolled, cum_slab)
                row = cum_t[e * SSD_HEAD_DIM:e * SSD_HEAD_DIM + 1, :]
                decay = jnp.exp(jnp.where(causal, col - row, -jnp.inf))
                scores = (cb[hd // (SSD_HEADS // SSD_GROUPS)] * decay).astype(BF16)
                pair.append(_dot(scores, xd_b[:, s * LANES:(s + 1) * LANES]))
            y_diag_slabs.append(jnp.where(lane_lo, pair[0], pair[1]))
        y_diag = jnp.concatenate(y_diag_slabs, axis=1)

        y_off_g = [_dot(cs_b[:, g * SSD_STATE:(g + 1) * SSD_STATE], state_b) for g in range(SSD_GROUPS)]
        y_off = jnp.where(grp0, y_off_g[0], y_off_g[1]) * jnp.exp(cum)

        xdd = (xd * jnp.exp(tot - cum)).astype(BF16)
        add_g = [_dot(bs[:, g * SSD_STATE:(g + 1) * SSD_STATE].T.astype(BF16), xdd) for g in range(SSD_GROUPS)]
        state_ref[...] = state_ref[...] * jnp.exp(tot) + jnp.where(grp0, add_g[0], add_g[1])

        gated = (y_diag + y_off + xs * vec(ROW_D_SKIP, D_SSD)) * _silu(pcol(OFF_SZ, D_SSD))
        g2 = gated * gated
        ss_all = jnp.sum(g2, axis=-1, keepdims=True)
        ss0 = jnp.sum(jnp.where(grp0, g2, 0.0), axis=-1, keepdims=True)
        inv0 = lax.rsqrt(ss0 * (1.0 / half) + SSD_NORM_EPS)
        inv1 = lax.rsqrt((ss_all - ss0) * (1.0 / half) + SSD_NORM_EPS)
        y_b = gated * jnp.where(grp0, inv0, inv1) * vec(ROW_SSD_NORM_G, D_SSD)
        ycat_ref[rows, D_CONV_A:D_CONV_A + D_SSD] = y_b.astype(BF16)

    for c in range(min(PROJ_LOOKAHEAD, n_chunks)):
        project(c)
    for c in range(n_chunks):
        if c + PROJ_LOOKAHEAD < n_chunks:
            project(c + PROJ_LOOKAHEAD)
        mix(c)
    ua_ref[0:HALO, :] = ua_ref[tile:tile + HALO, :]
    xbc_ref[0:HALO, :] = xbc_ref[tile:tile + HALO, :]

    scale = (QK_NOPE + QK_ROPE) ** -0.5 * LOG2E
    ang = pos_ref[...].astype(F32) * invf_ref[...]
    cos_t = jnp.cos(ang)
    sin_t = jnp.sin(ang)

    def rope_rows(t1, t2):
        return t1 * cos_t - t2 * sin_t, t2 * cos_t + t1 * sin_t

    c_qa = proj_ref[:, OFF_QA:OFF_QA + Q_LORA]
    qan = (c_qa * lax.rsqrt(jnp.mean(c_qa * c_qa, axis=-1, keepdims=True) + NORM_EPS)
           * vec(ROW_Q_NORM_G, Q_LORA)).astype(BF16)
    q_t = _dot_nt(wqbt_ref[...], qan)
    r0 = ROPE_ROW0
    for hd in range(MLA_HEADS):
        base = hd * HEAD_SLAB
        q1, q2 = rope_rows(q_t[base + r0:base + r0 + ROPE_HALF, :],
                           q_t[base + r0 + ROPE_HALF:base + r0 + QK_ROPE, :])
        slab = jnp.concatenate([q_t[base:base + r0, :], q1, q2,
                                q_t[base + r0 + QK_ROPE:base + HEAD_SLAB, :]], axis=0)
        qt_ref[base:base + HEAD_SLAB, :] = (slab * scale).astype(BF16)

    c_kv = proj_ref[:, OFF_KV:OFF_KV + KV_LORA]
    kvn = (c_kv * lax.rsqrt(jnp.mean(c_kv * c_kv, axis=-1, keepdims=True) + NORM_EPS)
           * vec(ROW_KV_NORM_G, KV_LORA)).astype(BF16)
    k1, k2 = rope_rows(ckrt_ref[r0:r0 + ROPE_HALF, :], ckrt_ref[r0 + ROPE_HALF:r0 + QK_ROPE, :])
    kr_nat = jnp.concatenate([ckrt_ref[0:r0, :], k1, k2, ckrt_ref[r0 + QK_ROPE:HEAD_SLAB, :]], axis=0).T
    k_nat = _dot(kvn, wk_ref[...])
    v_t = _dot_nt(wvt_ref[...], kvn)
    ones_row = lax.broadcasted_iota(jnp.int32, (V_SLAB, tile), 0) == ONES_ROW
    for hd in range(MLA_HEADS):
        base = hd * HEAD_SLAB
        kc_ref[j, :, base:base + HEAD_SLAB] = (k_nat[:, base:base + HEAD_SLAB] + kr_nat).astype(BF16)
        vt_ref[j, hd * V_SLAB:(hd + 1) * V_SLAB, :] = jnp.where(
            ones_row, 1.0, v_t[hd * V_SLAB:(hd + 1) * V_SLAB, :]).astype(BF16)

    m_ref[...] = jnp.full(m_ref.shape, -jnp.inf, F32)
    acc_ref[...] = jnp.zeros(acc_ref.shape, F32)
    kb = KEY_BLOCK

    def attend(i, masked):
        def span(sub):
            return (sub * kb if masked else 0), tile

        def scores(sub, hd):
            base = hd * HEAD_SLAB
            q0, q1 = span(sub)
            return _dot(kc_ref[i, sub * kb:(sub + 1) * kb, base:base + HEAD_SLAB],
                        qt_ref[base:base + HEAD_SLAB, q0:q1])

        steps = [(sub, hd) for sub in range(tile // kb) for hd in range(MLA_HEADS)]
        pending = [scores(*st) for st in steps[:SCORE_LOOKAHEAD]]
        for n, (sub, hd) in enumerate(steps):
            s_t = pending.pop(0)
            if n + SCORE_LOOKAHEAD < len(steps):
                pending.append(scores(*steps[n + SCORE_LOOKAHEAD]))
            vrows = slice(hd * V_SLAB, (hd + 1) * V_SLAB)
            q0, q1 = span(sub)
            if masked:
                visible = (lax.broadcasted_iota(jnp.int32, s_t.shape, 0)
                           <= lax.broadcasted_iota(jnp.int32, s_t.shape, 1))
                s_t = jnp.where(visible, s_t, -jnp.inf)
            m_old = m_ref[hd, :, q0:q1]
            m_new = jnp.maximum(m_old, jnp.max(s_t, axis=0, keepdims=True))
            p = jnp.exp2(s_t - m_new).astype(BF16)
            alpha = jnp.exp2(m_old - m_new)
            acc_ref[vrows, q0:q1] = (alpha * acc_ref[vrows, q0:q1]
                                     + _dot(vt_ref[i, vrows, sub * kb:(sub + 1) * kb], p))
            m_ref[hd, :, q0:q1] = m_new

    def body(i, carry):
        attend(i, False)
        return carry

    lax.fori_loop(0, j, body, 0)
    attend(j, True)

    for hd in range(MLA_HEADS):
        base = hd * V_SLAB
        yt_ref[hd * V_DIM:(hd + 1) * V_DIM, :] = (acc_ref[base:base + V_DIM, :]
                                                  / acc_ref[base + ONES_ROW:base + ONES_ROW + 1, :])
    c_z = proj_ref[:, OFF_CZ:OFF_CZ + D_MLA]
    ycat_ref[:, D_CONV_A + D_SSD:D_MIX] = (yt_ref[...].T * _silu(c_z)).astype(BF16)

    out = x_ref[...] + _dot(ycat_ref[...], w_out_ref[...])
    if final:
        out = (out * lax.rsqrt(jnp.mean(out * out, axis=-1, keepdims=True) + NORM_EPS)
               * vec(ROW_FINAL_G, D_MODEL))
    o_ref[...] = out


def _prep_layer(norm_g, w_in, conv_a_w, ssd_conv_w, ssd_conv_b, ssd_dt_bias, ssd_a_log, ssd_d,
                ssd_norm_g, mla_q_norm_g, w_qb, mla_kv_norm_g, w_kvb, w_out, final_norm_g):
    o_sdt = 4 * D_CONV_A + 2 * D_SSD + 2 * SSD_BC
    o_qa = o_sdt + SSD_HEADS
    o_kv = o_qa + Q_LORA
    o_kr = o_kv + KV_LORA
    o_cz = o_kr + QK_ROPE
    rep = lambda v: jnp.repeat(v, SSD_HEAD_DIM, axis=-1)
    w_nat = jnp.concatenate([
        w_in[:, 0:o_sdt], rep(w_in[:, o_sdt:o_qa]), w_in[:, o_qa:o_kr], w_in[:, o_cz:o_cz + D_MLA]],
        axis=1).astype(BF16)
    w_krt = jnp.pad(w_in[:, o_kr:o_cz].T, ((ROPE_ROW0, HEAD_SLAB - ROPE_ROW0 - QK_ROPE), (0, 0))).astype(BF16)
    wqb = w_qb.reshape(Q_LORA, MLA_HEADS, QK_NOPE + QK_ROPE)
    wqbt = jnp.pad(wqb, ((0, 0), (0, 0), (0, HEAD_SLAB - QK_NOPE - QK_ROPE))).reshape(
        Q_LORA, MLA_HEADS * HEAD_SLAB).T.astype(BF16)
    wkv = w_kvb.reshape(KV_LORA, MLA_HEADS, QK_NOPE + V_DIM)
    wk = jnp.pad(wkv[:, :, :QK_NOPE], ((0, 0), (0, 0), (0, HEAD_SLAB - QK_NOPE))).reshape(
        KV_LORA, MLA_HEADS * HEAD_SLAB).astype(BF16)
    wvt = jnp.pad(wkv[:, :, QK_NOPE:], ((0, 0), (0, 0), (0, V_SLAB - V_DIM))).reshape(
        KV_LORA, MLA_HEADS * V_SLAB).T.astype(BF16)

    def row(v):
        return jnp.pad(v.astype(F32), (0, D_MODEL - v.shape[0]))[None, :]

    rows = [row(norm_g)]
    rows += [row(conv_a_w[k]) for k in range(CONV_A_WIDTH)]
    rows += [row(ssd_conv_w[k]) for k in range(SSD_CONV_WIDTH)]
    rows += [row(ssd_conv_b), row(rep(ssd_dt_bias)), row(rep(ssd_a_log)), row(rep(ssd_d)),
             row(ssd_norm_g), row(mla_q_norm_g), row(mla_kv_norm_g), row(final_norm_g)]
    rows += [jnp.zeros((N_VEC_ROWS - len(rows), D_MODEL), F32)]
    vecs = jnp.concatenate(rows, axis=0)
    return vecs, w_nat, w_krt, wqbt, wk, wvt, w_out.astype(BF16)


def _vmem_limit(tile, n_tiles):
    f32, bf16 = 4, 2
    blocks = 2 * (2 * tile * D_MODEL * f32)
    weights = 2 * bf16 * (D_MODEL * N_PROJ + HEAD_SLAB * D_MODEL + Q_LORA * MLA_HEADS * HEAD_SLAB
                          + KV_LORA * MLA_HEADS * (HEAD_SLAB + V_SLAB) + D_MIX * D_MODEL)
    scratch = (tile * N_PROJ * f32 + HEAD_SLAB * tile * f32 + (tile + HALO) * (D_CONV_A + SSD_CONV_DIM) * f32
               + SSD_STATE * D_SSD * f32 + n_tiles * tile * MLA_HEADS * (HEAD_SLAB + V_SLAB) * bf16
               + MLA_HEADS * HEAD_SLAB * tile * bf16 + MLA_HEADS * V_SLAB * tile * f32
               + D_MLA * tile * f32 + tile * D_MIX * bf16)
    temporaries = 2 * tile * N_PROJ * f32
    return min(blocks + weights + scratch + temporaries, V7X_VMEM_BYTES - 8 * 1024 * 1024)


def _layer_call(x, pos3, invf, params, *, final):
    batch, seq, d_model = x.shape
    tile = SEQ_TILE
    n_tiles = seq // tile
    vecs, w_nat, w_krt, wqbt, wk, wvt, w_out = params
    const = lambda shape: pl.BlockSpec(shape, lambda b, j: (0,) * len(shape))
    return pl.pallas_call(
        functools.partial(_layer_kernel, final=final),
        out_shape=jax.ShapeDtypeStruct(x.shape, F32),
        grid=(batch, n_tiles),
        in_specs=[
            pl.BlockSpec((None, 1, tile), lambda b, j: (b, 0, j)),
            pl.BlockSpec((None, tile, d_model), lambda b, j: (b, j, 0)),
            const(invf.shape), const(vecs.shape), const(w_nat.shape), const(w_krt.shape),
            const(wqbt.shape), const(wk.shape), const(wvt.shape), const(w_out.shape),
        ],
        out_specs=pl.BlockSpec((None, tile, d_model), lambda b, j: (b, j, 0)),
        scratch_shapes=[
            pltpu.VMEM((tile, N_PROJ), F32),
            pltpu.VMEM((HEAD_SLAB, tile), F32),
            pltpu.VMEM((tile + HALO, D_CONV_A), F32),
            pltpu.VMEM((tile + HALO, SSD_CONV_DIM), F32),
            pltpu.VMEM((SSD_STATE, D_SSD), F32),
            pltpu.VMEM((n_tiles, tile, MLA_HEADS * HEAD_SLAB), BF16),
            pltpu.VMEM((n_tiles, MLA_HEADS * V_SLAB, tile), BF16),
            pltpu.VMEM((MLA_HEADS * HEAD_SLAB, tile), BF16),
            pltpu.VMEM((MLA_HEADS * V_SLAB, tile), F32),
            pltpu.VMEM((MLA_HEADS, 1, tile), F32),
            pltpu.VMEM((D_MLA, tile), F32),
            pltpu.VMEM((tile, D_MIX), BF16),
        ],
        compiler_params=pltpu.CompilerParams(
            dimension_semantics=("arbitrary", "arbitrary"),
            vmem_limit_bytes=_vmem_limit(tile, n_tiles)),
        name="hybrid_layer_final" if final else "hybrid_layer",
    )(pos3, x, invf, vecs, w_nat, w_krt, wqbt, wk, wvt, w_out)


def kernel(x, positions, norm_g, w_in, conv_a_w, ssd_conv_w, ssd_conv_b, ssd_dt_bias, ssd_a_log, ssd_d,
           ssd_norm_g, mla_q_norm_g, w_qb, mla_kv_norm_g, w_kvb, w_out, final_norm_g):
    batch, seq, _ = x.shape
    depth = norm_g.shape[0]
    assert seq % SEQ_TILE == 0 and SEQ_TILE % SSD_CHUNK == 0 and SEQ_TILE % KEY_BLOCK == 0
    inv_freq = ROPE_BASE ** (-jnp.arange(0, QK_ROPE, 2, dtype=F32) / QK_ROPE)
    invf = jnp.broadcast_to(inv_freq[:, None], (ROPE_HALF, SEQ_TILE))
    pos3 = positions.reshape(batch, 1, seq)
    for l in range(depth):
        params = _prep_layer(norm_g[l], w_in[l], conv_a_w[l], ssd_conv_w[l], ssd_conv_b[l], ssd_dt_bias[l],
                             ssd_a_log[l], ssd_d[l], ssd_norm_g[l], mla_q_norm_g[l], w_qb[l],
                             mla_kv_norm_g[l], w_kvb[l], w_out[l], final_norm_g)
        x = _layer_call(x, pos3, invf, params, final=(l == depth - 1))
    return x
```

```python
import functools

import jax
import jax.numpy as jnp
from jax import lax
from jax.experimental import pallas as pl
from jax.experimental.pallas import tpu as pltpu

F32 = jnp.float32
BF16 = jnp.bfloat16

D_MODEL = 1024
D_CONV_A = 256
CONV_A_WIDTH = 3
SSD_HEADS = 6
SSD_HEAD_DIM = 64
D_SSD = SSD_HEADS * SSD_HEAD_DIM
SSD_GROUPS = 2
SSD_STATE = 128
SSD_CONV_WIDTH = 4
SSD_BC = SSD_GROUPS * SSD_STATE
SSD_CONV_DIM = D_SSD + 2 * SSD_BC
SSD_NORM_EPS = 1e-5
MLA_HEADS = 6
Q_LORA = 256
KV_LORA = 128
QK_NOPE = 64
QK_ROPE = 32
V_DIM = 64
D_MLA = MLA_HEADS * V_DIM
ROPE_BASE = 10000.0
D_MIX = D_CONV_A + D_SSD + D_MLA
NORM_EPS = 1e-6

LANES = 128
SUBLANES = 8
V7X_VMEM_BYTES = 64 * 1024 * 1024

SEQ_TILE = 512
SSD_CHUNK = 128
HEAD_SLAB = LANES
ROPE_HALF = QK_ROPE // 2
ROPE_ROW0 = QK_NOPE
ONES_ROW = V_DIM
V_SLAB = 80
KEY_BLOCK = 128
SCORE_LOOKAHEAD = 4
PROJ_GROUP = 512
DIAG_STAGES_PER_MIX_STAGE = 3
LOG2E = 1.4426950408889634
HALO = SUBLANES

OFF_A = 0
OFF_SZ = OFF_A + 4 * D_CONV_A
OFF_XBC = OFF_SZ + D_SSD
OFF_SDT = OFF_XBC + SSD_CONV_DIM
OFF_QA = OFF_SDT + SSD_HEADS
OFF_KV = OFF_QA + Q_LORA
OFF_KR = OFF_KV + KV_LORA
OFF_CZ = OFF_KR + QK_ROPE
IN_COLS = OFF_CZ + D_MLA
N_PROJ = -(-IN_COLS // LANES) * LANES

ROW_NORM_G = 0
ROW_CONV_A = 1
ROW_SSD_CONV = ROW_CONV_A + CONV_A_WIDTH
ROW_SSD_CONV_B = ROW_SSD_CONV + SSD_CONV_WIDTH
ROW_DT_BIAS = ROW_SSD_CONV_B + 1
ROW_A_LOG = ROW_DT_BIAS + 1
ROW_D_SKIP = ROW_A_LOG + 1
ROW_SSD_NORM_G = ROW_D_SKIP + 1
ROW_Q_NORM_G = ROW_SSD_NORM_G + 1
ROW_KV_NORM_G = ROW_Q_NORM_G + 1
ROW_FINAL_G = ROW_KV_NORM_G + 1
N_VEC_ROWS = 16


def _silu(v):
    return v * (1.0 / (1.0 + jnp.exp(-v)))


def _softplus(v):
    return jnp.maximum(v, 0.0) + jnp.log1p(jnp.exp(-jnp.abs(v)))


def _split3(v):
    hi = v.astype(BF16)
    r1 = v - hi.astype(F32)
    mid = r1.astype(BF16)
    lo = (r1 - mid.astype(F32)).astype(BF16)
    return hi, mid, lo


def _dot(a, b):
    return jnp.dot(a, b, preferred_element_type=F32)


def _dot_nt(a, b):
    return lax.dot_general(a, b, (((1,), (1,)), ((), ())), preferred_element_type=F32)


def _interleave(*stage_iters):
    live = [it if isinstance(it, tuple) else (it, 1) for it in stage_iters]
    while live:
        for entry in list(live):
            it, per_round = entry
            for _ in range(per_round):
                if next(it, StopIteration) is StopIteration:
                    live.remove(entry)
                    break


def _layer_kernel(pos_ref, x_ref, invf_ref, vec_ref, w_in_ref, wqbt_ref, wk_ref,
                  wvt_ref, w_out_ref, o_ref,
                  ua_ref, xbc_ref, state_ref, kc_ref, vt_ref, qt_ref, acc_ref,
                  m_ref, yt_ref, ycat_ref, *proj_refs, final):
    j = pl.program_id(1)
    tile = x_ref.shape[0]

    def ptile(off, width):
        return jnp.concatenate([p[:, off:off + width] for p in proj_refs], axis=0)

    def vec(row, width):
        return vec_ref[row:row + 1, 0:width]

    L = SSD_CHUNK
    n_chunks = tile // L

    @pl.when(j == 0)
    def _():
        ua_ref[0:HALO, :] = jnp.zeros((HALO, D_CONV_A), F32)
        xbc_ref[0:HALO, :] = jnp.zeros((HALO, SSD_CONV_DIM), F32)
        state_ref[...] = jnp.zeros(state_ref.shape, F32)

    row_i = lax.broadcasted_iota(jnp.int32, (L, L), 0)
    col_i = lax.broadcasted_iota(jnp.int32, (L, L), 1)
    causal = row_i >= col_i
    tril = causal.astype(BF16)
    lane_lo = lax.broadcasted_iota(jnp.int32, (L, LANES), 1) < SSD_HEAD_DIM
    half = D_SSD // SSD_GROUPS
    grp0 = lax.broadcasted_iota(jnp.int32, (1, D_SSD), 1) < half
    a_neg = -jnp.exp(vec(ROW_A_LOG, D_SSD))
    head_of_lane = lax.shift_right_logical(lax.broadcasted_iota(jnp.int32, (LANES, D_SSD), 1),
                                           jnp.int32(SSD_HEAD_DIM.bit_length() - 1))
    head_expand = (lax.broadcasted_iota(jnp.int32, (LANES, D_SSD), 0) == head_of_lane).astype(BF16)

    def project(c):
        rows = slice(c * L, (c + 1) * L)
        x = x_ref[rows, :]
        h = x * lax.rsqrt(jnp.mean(x * x, axis=-1, keepdims=True) + NORM_EPS) * vec(ROW_NORM_G, D_MODEL)
        hb = h.astype(BF16)
        yield
        for g0 in range(0, N_PROJ, PROJ_GROUP):
            g1 = min(g0 + PROJ_GROUP, N_PROJ)
            proj_refs[c][:, g0:g1] = _dot(hb, w_in_ref[:, g0:g1])
            yield

    def mix(c):
        rows = slice(c * L, (c + 1) * L)
        lo = HALO + c * L

        def pcol(off, width):
            return proj_refs[c][:, off:off + width]

        ua_ref[lo:lo + L, :] = pcol(OFF_A + 2 * D_CONV_A, D_CONV_A) * pcol(OFF_A, D_CONV_A)
        conv_a = vec(ROW_CONV_A + CONV_A_WIDTH - 1, D_CONV_A) * ua_ref[lo:lo + L, :]
        for k in range(1, CONV_A_WIDTH):
            conv_a += vec(ROW_CONV_A + CONV_A_WIDTH - 1 - k, D_CONV_A) * ua_ref[lo - k:lo - k + L, :]
        ycat_ref[rows, 0:D_CONV_A] = (pcol(OFF_A + D_CONV_A, D_CONV_A) * conv_a
                                      * _silu(pcol(OFF_A + 3 * D_CONV_A, D_CONV_A))).astype(BF16)
        yield

        xbc_ref[lo:lo + L, :] = pcol(OFF_XBC, SSD_CONV_DIM)
        conv_b = vec(ROW_SSD_CONV + SSD_CONV_WIDTH - 1, SSD_CONV_DIM) * xbc_ref[lo:lo + L, :]
        for k in range(1, SSD_CONV_WIDTH):
            conv_b += (vec(ROW_SSD_CONV + SSD_CONV_WIDTH - 1 - k, SSD_CONV_DIM)
                       * xbc_ref[lo - k:lo - k + L, :])
        xbc = _silu(conv_b + vec(ROW_SSD_CONV_B, SSD_CONV_DIM))
        xs = xbc[:, 0:D_SSD]
        bs = xbc[:, D_SSD:D_SSD + SSD_BC]
        cs = xbc[:, D_SSD + SSD_BC:SSD_CONV_DIM]
        yield
        d_hi, d_mid, d_lo = _split3(pcol(OFF_SDT, LANES))
        dt_raw = _dot(d_hi, head_expand) + _dot(d_mid, head_expand) + _dot(d_lo, head_expand)
        dt_c = _softplus(dt_raw + vec(ROW_DT_BIAS, D_SSD))
        la = dt_c * a_neg
        p_hi, p_mid, p_lo = _split3(la)
        yield
        cum = _dot(tril, p_hi) + _dot(tril, p_mid) + _dot(tril, p_lo)
        tot = cum[L - 1:L, :]
        xd = xs * dt_c
        xd_b = xd.astype(BF16)
        bs_b = bs.astype(BF16)
        cs_b = cs.astype(BF16)
        state_b = state_ref[...].astype(BF16)

        cb = [_dot_nt(cs_b[:, g * SSD_STATE:(g + 1) * SSD_STATE],
                      bs_b[:, g * SSD_STATE:(g + 1) * SSD_STATE]) for g in range(SSD_GROUPS)]
        yield
        y_diag_slabs = []
        for s in range(D_SSD // LANES):
            cum_slab = cum[:, s * LANES:(s + 1) * LANES]
            cum_rolled = pltpu.roll(cum_slab, SSD_HEAD_DIM, 1)
            cum_t = cum_slab.T
            pair = []
            for e in range(2):
                hd = 2 * s + e
                col = jnp.where(lane_lo, cum_slab, cum_rolled) if e == 0 else jnp.where(lane_lo, cum_rolled, cum_slab)
                row = cum_t[e * SSD_HEAD_DIM:e * SSD_HEAD_DIM + 1, :]
                decay = jnp.exp(jnp.where(causal, col - row, -jnp.inf))
                scores = (cb[hd // (SSD_HEADS // SSD_GROUPS)] * decay).astype(BF16)
                pair.append(_dot(scores, xd_b[:, s * LANES:(s + 1) * LANES]))
            y_diag_slabs.append(jnp.where(lane_lo, pair[0], pair[1]))
            yield
        y_diag = jnp.concatenate(y_diag_slabs, axis=1)

        y_off_g = [_dot(cs_b[:, g * SSD_STATE:(g + 1) * SSD_STATE], state_b) for g in range(SSD_GROUPS)]
        y_off = jnp.where(grp0, y_off_g[0], y_off_g[1]) * jnp.exp(cum)

        xdd = (xd * jnp.exp(tot - cum)).astype(BF16)
        add_g = [_dot(bs[:, g * SSD_STATE:(g + 1) * SSD_STATE].T.astype(BF16), xdd) for g in range(SSD_GROUPS)]
        state_ref[...] = state_ref[...] * jnp.exp(tot) + jnp.where(grp0, add_g[0], add_g[1])
        yield

        gated = (y_diag + y_off + xs * vec(ROW_D_SKIP, D_SSD)) * _silu(pcol(OFF_SZ, D_SSD))
        g2 = gated * gated
        ss_all = jnp.sum(g2, axis=-1, keepdims=True)
        ss0 = jnp.sum(jnp.where(grp0, g2, 0.0), axis=-1, keepdims=True)
        inv0 = lax.rsqrt(ss0 * (1.0 / half) + SSD_NORM_EPS)
        inv1 = lax.rsqrt((ss_all - ss0) * (1.0 / half) + SSD_NORM_EPS)
        y_b = gated * jnp.where(grp0, inv0, inv1) * vec(ROW_SSD_NORM_G, D_SSD)
        ycat_ref[rows, D_CONV_A:D_CONV_A + D_SSD] = y_b.astype(BF16)
        yield

    scale = (QK_NOPE + QK_ROPE) ** -0.5 * LOG2E
    r0 = ROPE_ROW0
    kb = KEY_BLOCK

    def attend_prepare():
        ang = pos_ref[...].astype(F32) * invf_ref[...]
        cos_t = jnp.cos(ang)
        sin_t = jnp.sin(ang)

        def rope_rows(t1, t2):
            return t1 * cos_t - t2 * sin_t, t2 * cos_t + t1 * sin_t

        c_qa = ptile(OFF_QA, Q_LORA)
        qan = (c_qa * lax.rsqrt(jnp.mean(c_qa * c_qa, axis=-1, keepdims=True) + NORM_EPS)
               * vec(ROW_Q_NORM_G, Q_LORA)).astype(BF16)
        q_t = _dot_nt(wqbt_ref[...], qan)
        yield
        for hd in range(MLA_HEADS):
            base = hd * HEAD_SLAB
            q1, q2 = rope_rows(q_t[base + r0:base + r0 + ROPE_HALF, :],
                               q_t[base + r0 + ROPE_HALF:base + r0 + QK_ROPE, :])
            slab = jnp.concatenate([q_t[base:base + r0, :], q1, q2,
                                    q_t[base + r0 + QK_ROPE:base + HEAD_SLAB, :]], axis=0)
            qt_ref[base:base + HEAD_SLAB, :] = (slab * scale).astype(BF16)
        yield

        c_kv = ptile(OFF_KV, KV_LORA)
        kvn = (c_kv * lax.rsqrt(jnp.mean(c_kv * c_kv, axis=-1, keepdims=True) + NORM_EPS)
               * vec(ROW_KV_NORM_G, KV_LORA)).astype(BF16)
        kr_slab0 = OFF_KR // LANES * LANES
        kr_lane = lax.broadcasted_iota(jnp.int32, (tile, LANES), 1)
        kr_rolled = pltpu.roll(ptile(kr_slab0, LANES), r0 - (OFF_KR - kr_slab0), 1)
        ckr_t = jnp.where((kr_lane >= r0) & (kr_lane < r0 + QK_ROPE), kr_rolled, 0.0).T
        k1, k2 = rope_rows(ckr_t[r0:r0 + ROPE_HALF, :], ckr_t[r0 + ROPE_HALF:r0 + QK_ROPE, :])
        kr_nat = jnp.concatenate([ckr_t[0:r0, :], k1, k2, ckr_t[r0 + QK_ROPE:HEAD_SLAB, :]], axis=0).T
        yield
        k_nat = _dot(kvn, wk_ref[...])
        v_t = _dot_nt(wvt_ref[...], kvn)
        ones_row = lax.broadcasted_iota(jnp.int32, (V_SLAB, tile), 0) == ONES_ROW
        for hd in range(MLA_HEADS):
            base = hd * HEAD_SLAB
            kc_ref[j, :, base:base + HEAD_SLAB] = (k_nat[:, base:base + HEAD_SLAB] + kr_nat).astype(BF16)
            vt_ref[j, hd * V_SLAB:(hd + 1) * V_SLAB, :] = jnp.where(
                ones_row, 1.0, v_t[hd * V_SLAB:(hd + 1) * V_SLAB, :]).astype(BF16)
        m_ref[...] = jnp.full(m_ref.shape, -jnp.inf, F32)
        acc_ref[...] = jnp.zeros(acc_ref.shape, F32)
        yield

    def attend(i, masked):
        def span(sub):
            return (sub * kb if masked else 0), tile

        def scores(sub, hd):
            base = hd * HEAD_SLAB
            q0, q1 = span(sub)
            return _dot(kc_ref[i, sub * kb:(sub + 1) * kb, base:base + HEAD_SLAB],
                        qt_ref[base:base + HEAD_SLAB, q0:q1])

        steps = [(sub, hd) for sub in range(tile // kb) for hd in range(MLA_HEADS)]
        pending = [scores(*st) for st in steps[:SCORE_LOOKAHEAD]]
        for n, (sub, hd) in enumerate(steps):
            s_t = pending.pop(0)
            if n + SCORE_LOOKAHEAD < len(steps):
                pending.append(scores(*steps[n + SCORE_LOOKAHEAD]))
            vrows = slice(hd * V_SLAB, (hd + 1) * V_SLAB)
            q0, q1 = span(sub)
            if masked:
                visible = (lax.broadcasted_iota(jnp.int32, s_t.shape, 0)
                           <= lax.broadcasted_iota(jnp.int32, s_t.shape, 1))
                s_t = jnp.where(visible, s_t, -jnp.inf)
            m_old = m_ref[hd, :, q0:q1]
            m_new = jnp.maximum(m_old, jnp.max(s_t, axis=0, keepdims=True))
            p = jnp.exp2(s_t - m_new).astype(BF16)
            alpha = jnp.exp2(m_old - m_new)
            acc_ref[vrows, q0:q1] = (alpha * acc_ref[vrows, q0:q1]
                                     + _dot(vt_ref[i, vrows, sub * kb:(sub + 1) * kb], p))
            m_ref[hd, :, q0:q1] = m_new
            yield

    def attend_diagonal():
        yield from attend_prepare()
        yield from attend(j, True)

    _interleave(project(0))
    for c in range(n_chunks - 1):
        _interleave(project(c + 1), mix(c))
    _interleave((attend_diagonal(), DIAG_STAGES_PER_MIX_STAGE), mix(n_chunks - 1))
    ua_ref[0:HALO, :] = ua_ref[tile:tile + HALO, :]
    xbc_ref[0:HALO, :] = xbc_ref[tile:tile + HALO, :]

    def body(i, carry):
        _interleave(attend(i, False))
        return carry

    lax.fori_loop(0, j, body, 0)

    for hd in range(MLA_HEADS):
        base = hd * V_SLAB
        yt_ref[hd * V_DIM:(hd + 1) * V_DIM, :] = (acc_ref[base:base + V_DIM, :]
                                                  / acc_ref[base + ONES_ROW:base + ONES_ROW + 1, :])
    c_z = ptile(OFF_CZ, D_MLA)
    ycat_ref[:, D_CONV_A + D_SSD:D_MIX] = (yt_ref[...].T * _silu(c_z)).astype(BF16)

    out = x_ref[...] + _dot(ycat_ref[...], w_out_ref[...])
    if final:
        out = (out * lax.rsqrt(jnp.mean(out * out, axis=-1, keepdims=True) + NORM_EPS)
               * vec(ROW_FINAL_G, D_MODEL))
    o_ref[...] = out


def _prep_layer(norm_g, w_in, conv_a_w, ssd_conv_w, ssd_conv_b, ssd_dt_bias, ssd_a_log, ssd_d,
                ssd_norm_g, mla_q_norm_g, w_qb, mla_kv_norm_g, w_kvb, w_out, final_norm_g):
    rep = lambda v: jnp.repeat(v, SSD_HEAD_DIM, axis=-1)
    w_nat = jnp.pad(w_in.astype(BF16), ((0, 0), (0, N_PROJ - IN_COLS)))
    wqb = w_qb.reshape(Q_LORA, MLA_HEADS, QK_NOPE + QK_ROPE)
    wqbt = jnp.pad(wqb, ((0, 0), (0, 0), (0, HEAD_SLAB - QK_NOPE - QK_ROPE))).reshape(
        Q_LORA, MLA_HEADS * HEAD_SLAB).T.astype(BF16)
    wkv = w_kvb.reshape(KV_LORA, MLA_HEADS, QK_NOPE + V_DIM)
    wk = jnp.pad(wkv[:, :, :QK_NOPE], ((0, 0), (0, 0), (0, HEAD_SLAB - QK_NOPE))).reshape(
        KV_LORA, MLA_HEADS * HEAD_SLAB).astype(BF16)
    wvt = jnp.pad(wkv[:, :, QK_NOPE:], ((0, 0), (0, 0), (0, V_SLAB - V_DIM))).reshape(
        KV_LORA, MLA_HEADS * V_SLAB).T.astype(BF16)

    def row(v):
        return jnp.pad(v.astype(F32), (0, D_MODEL - v.shape[0]))[None, :]

    rows = [row(norm_g)]
    rows += [row(conv_a_w[k]) for k in range(CONV_A_WIDTH)]
    rows += [row(ssd_conv_w[k]) for k in range(SSD_CONV_WIDTH)]
    rows += [row(ssd_conv_b), row(rep(ssd_dt_bias)), row(rep(ssd_a_log)), row(rep(ssd_d)),
             row(ssd_norm_g), row(mla_q_norm_g), row(mla_kv_norm_g), row(final_norm_g)]
    rows += [jnp.zeros((N_VEC_ROWS - len(rows), D_MODEL), F32)]
    vecs = jnp.concatenate(rows, axis=0)
    return vecs, w_nat, wqbt, wk, wvt, w_out.astype(BF16)


def _vmem_limit(tile, n_tiles):
    f32, bf16 = 4, 2
    blocks = 2 * (2 * tile * D_MODEL * f32)
    weights = 2 * bf16 * (D_MODEL * N_PROJ + Q_LORA * MLA_HEADS * HEAD_SLAB
                          + KV_LORA * MLA_HEADS * (HEAD_SLAB + V_SLAB) + D_MIX * D_MODEL)
    scratch = (tile * N_PROJ * f32 + (tile + HALO) * (D_CONV_A + SSD_CONV_DIM) * f32
               + SSD_STATE * D_SSD * f32 + n_tiles * tile * MLA_HEADS * (HEAD_SLAB + V_SLAB) * bf16
               + MLA_HEADS * HEAD_SLAB * tile * bf16 + MLA_HEADS * V_SLAB * tile * f32
               + D_MLA * tile * f32 + tile * D_MIX * bf16)
    temporaries = 2 * tile * N_PROJ * f32
    return min(blocks + weights + scratch + temporaries, V7X_VMEM_BYTES - 8 * 1024 * 1024)


def _layer_call(x, pos3, invf, params, *, final):
    batch, seq, d_model = x.shape
    tile = SEQ_TILE
    n_tiles = seq // tile
    vecs, w_nat, wqbt, wk, wvt, w_out = params
    const = lambda shape: pl.BlockSpec(shape, lambda b, j: (0,) * len(shape))
    return pl.pallas_call(
        functools.partial(_layer_kernel, final=final),
        out_shape=jax.ShapeDtypeStruct(x.shape, F32),
        grid=(batch, n_tiles),
        in_specs=[
            pl.BlockSpec((None, 1, tile), lambda b, j: (b, 0, j)),
            pl.BlockSpec((None, tile, d_model), lambda b, j: (b, j, 0)),
            const(invf.shape), const(vecs.shape), const(w_nat.shape),
            const(wqbt.shape), const(wk.shape), const(wvt.shape), const(w_out.shape),
        ],
        out_specs=pl.BlockSpec((None, tile, d_model), lambda b, j: (b, j, 0)),
        scratch_shapes=[
            pltpu.VMEM((tile + HALO, D_CONV_A), F32),
            pltpu.VMEM((tile + HALO, SSD_CONV_DIM), F32),
            pltpu.VMEM((SSD_STATE, D_SSD), F32),
            pltpu.VMEM((n_tiles, tile, MLA_HEADS * HEAD_SLAB), BF16),
            pltpu.VMEM((n_tiles, MLA_HEADS * V_SLAB, tile), BF16),
            pltpu.VMEM((MLA_HEADS * HEAD_SLAB, tile), BF16),
            pltpu.VMEM((MLA_HEADS * V_SLAB, tile), F32),
            pltpu.VMEM((MLA_HEADS, 1, tile), F32),
            pltpu.VMEM((D_MLA, tile), F32),
            pltpu.VMEM((tile, D_MIX), BF16),
        ] + [pltpu.VMEM((SSD_CHUNK, N_PROJ), F32)] * (tile // SSD_CHUNK),
        compiler_params=pltpu.CompilerParams(
            dimension_semantics=("arbitrary", "arbitrary"),
            vmem_limit_bytes=_vmem_limit(tile, n_tiles)),
        name="hybrid_layer_final" if final else "hybrid_layer",
    )(pos3, x, invf, vecs, w_nat, wqbt, wk, wvt, w_out)


def kernel(x, positions, norm_g, w_in, conv_a_w, ssd_conv_w, ssd_conv_b, ssd_dt_bias, ssd_a_log, ssd_d,
           ssd_norm_g, mla_q_norm_g, w_qb, mla_kv_norm_g, w_kvb, w_out, final_norm_g):
    batch, seq, _ = x.shape
    depth = norm_g.shape[0]
    assert seq % SEQ_TILE == 0 and SEQ_TILE % SSD_CHUNK == 0 and SEQ_TILE % KEY_BLOCK == 0
    inv_freq = ROPE_BASE ** (-jnp.arange(0, QK_ROPE, 2, dtype=F32) / QK_ROPE)
    invf = jnp.broadcast_to(inv_freq[:, None], (ROPE_HALF, SEQ_TILE))
    pos3 = positions.reshape(batch, 1, seq)
    for l in range(depth):
        params = _prep_layer(norm_g[l], w_in[l], conv_a_w[l], ssd_conv_w[l], ssd_conv_b[l], ssd_dt_bias[l],
                             ssd_a_log[l], ssd_d[l], ssd_norm_g[l], mla_q_norm_g[l], w_qb[l],
                             mla_kv_norm_g[l], w_kvb[l], w_out[l], final_norm_g)
        x = _layer_call(x, pos3, invf, params, final=(l == depth - 1))
    return x
```

```python
import functools

import jax
import jax.numpy as jnp
from jax import lax
from jax.experimental import pallas as pl
from jax.experimental.pallas import tpu as pltpu

F32 = jnp.float32
BF16 = jnp.bfloat16

D_MODEL = 1024
D_CONV_A = 256
CONV_A_WIDTH = 3
SSD_HEADS = 6
SSD_HEAD_DIM = 64
D_SSD = SSD_HEADS * SSD_HEAD_DIM
SSD_GROUPS = 2
SSD_STATE = 128
SSD_CONV_WIDTH = 4
SSD_BC = SSD_GROUPS * SSD_STATE
SSD_CONV_DIM = D_SSD + 2 * SSD_BC
SSD_NORM_EPS = 1e-5
MLA_HEADS = 6
Q_LORA = 256
KV_LORA = 128
QK_NOPE = 64
QK_ROPE = 32
V_DIM = 64
D_MLA = MLA_HEADS * V_DIM
ROPE_BASE = 10000.0
D_MIX = D_CONV_A + D_SSD + D_MLA
NORM_EPS = 1e-6

LANES = 128
SUBLANES = 8
V7X_VMEM_BYTES = 64 * 1024 * 1024

SEQ_TILE = 512
SSD_CHUNK = 128
HEAD_SLAB = LANES
ROPE_HALF = QK_ROPE // 2
ROPE_ROW0 = QK_NOPE
ONES_ROW = V_DIM
V_SLAB = 80
KEY_BLOCK = 128
SCORE_LOOKAHEAD = 4
PROJ_GROUP = 512
DIAG_STAGES_PER_MIX_STAGE = 3
LOG2E = 1.4426950408889634
HALO = SUBLANES
CONVERT_ROWS = 256

OFF_A = 0
OFF_SZ = OFF_A + 4 * D_CONV_A
OFF_XBC = OFF_SZ + D_SSD
OFF_SDT = OFF_XBC + SSD_CONV_DIM
OFF_QA = OFF_SDT + SSD_HEADS
OFF_KV = OFF_QA + Q_LORA
OFF_KR = OFF_KV + KV_LORA
OFF_CZ = OFF_KR + QK_ROPE
IN_COLS = OFF_CZ + D_MLA
N_PROJ = -(-IN_COLS // LANES) * LANES

ROW_NORM_G = 0
ROW_CONV_A = 1
ROW_SSD_CONV = ROW_CONV_A + CONV_A_WIDTH
ROW_SSD_CONV_B = ROW_SSD_CONV + SSD_CONV_WIDTH
ROW_DT_BIAS = ROW_SSD_CONV_B + 1
ROW_A_LOG = ROW_DT_BIAS + 1
ROW_D_SKIP = ROW_A_LOG + 1
ROW_SSD_NORM_G = ROW_D_SKIP + 1
ROW_Q_NORM_G = ROW_SSD_NORM_G + 1
ROW_KV_NORM_G = ROW_Q_NORM_G + 1
ROW_FINAL_G = ROW_KV_NORM_G + 1
N_VEC_ROWS = 16


def _silu(v):
    return v * (1.0 / (1.0 + jnp.exp(-v)))


def _softplus(v):
    return jnp.maximum(v, 0.0) + jnp.log1p(jnp.exp(-jnp.abs(v)))


def _split3(v):
    hi = v.astype(BF16)
    r1 = v - hi.astype(F32)
    mid = r1.astype(BF16)
    lo = (r1 - mid.astype(F32)).astype(BF16)
    return hi, mid, lo


def _dot(a, b):
    return jnp.dot(a, b, preferred_element_type=F32)


def _dot_pieces(a, b):
    if isinstance(a, tuple):
        return functools.reduce(lambda acc, piece: acc + _dot(piece, b), a[1:], _dot(a[0], b))
    return functools.reduce(lambda acc, piece: acc + _dot(a, piece), b[1:], _dot(a, b[0]))


def _dot_nt(a, b):
    return lax.dot_general(a, b, (((1,), (1,)), ((), ())), preferred_element_type=F32)


def _interleave(*stage_iters):
    live = [it if isinstance(it, tuple) else (it, 1) for it in stage_iters]
    while live:
        for entry in list(live):
            it, per_round = entry
            for _ in range(per_round):
                if next(it, StopIteration) is StopIteration:
                    live.remove(entry)
                    break


def _layer_kernel(pos_ref, x_ref, invf_ref, vec_ref, w_in_ref, wqbt_ref, wk_ref,
                  wvt_ref, w_out_ref, o_ref,
                  ua_ref, xbc_ref, state_ref, kc_ref, vt_ref, qt_ref, acc_ref,
                  m_ref, yt_ref, ycat_ref, *proj_refs, final):
    j = pl.program_id(1)
    tile = x_ref.shape[0]

    def ptile(off, width):
        return jnp.concatenate([p[:, off:off + width] for p in proj_refs], axis=0)

    def vec(row, width):
        return vec_ref[row:row + 1, 0:width]

    L = SSD_CHUNK
    n_chunks = tile // L

    @pl.when(j == 0)
    def _():
        ua_ref[0:HALO, :] = jnp.zeros((HALO, D_CONV_A), F32)
        xbc_ref[0:HALO, :] = jnp.zeros((HALO, SSD_CONV_DIM), F32)
        state_ref[...] = jnp.zeros(state_ref.shape, F32)

    row_i = lax.broadcasted_iota(jnp.int32, (L, L), 0)
    col_i = lax.broadcasted_iota(jnp.int32, (L, L), 1)
    causal = row_i >= col_i
    tril = causal.astype(BF16)
    lane_lo = lax.broadcasted_iota(jnp.int32, (L, LANES), 1) < SSD_HEAD_DIM
    half = D_SSD // SSD_GROUPS
    grp0 = lax.broadcasted_iota(jnp.int32, (1, D_SSD), 1) < half
    head_lane = lax.broadcasted_iota(jnp.int32, (1, LANES), 1) < SSD_HEADS
    a_neg2 = -jnp.exp(vec(ROW_A_LOG, LANES)) * LOG2E
    head_of_lane = lax.shift_right_logical(lax.broadcasted_iota(jnp.int32, (LANES, D_SSD), 1),
                                           jnp.int32(SSD_HEAD_DIM.bit_length() - 1))
    head_expand = (lax.broadcasted_iota(jnp.int32, (LANES, D_SSD), 0) == head_of_lane).astype(BF16)

    def project(c):
        rows = slice(c * L, (c + 1) * L)
        x = x_ref[rows, :]
        h = x * lax.rsqrt(jnp.mean(x * x, axis=-1, keepdims=True) + NORM_EPS) * vec(ROW_NORM_G, D_MODEL)
        hb = h.astype(BF16)
        yield
        for g0 in range(0, N_PROJ, PROJ_GROUP):
            g1 = min(g0 + PROJ_GROUP, N_PROJ)
            proj_refs[c][:, g0:g1] = _dot(hb, w_in_ref[:, g0:g1])
            yield

    def mix(c):
        rows = slice(c * L, (c + 1) * L)
        lo = HALO + c * L

        def pcol(off, width):
            return proj_refs[c][:, off:off + width]

        ua_ref[lo:lo + L, :] = pcol(OFF_A + 2 * D_CONV_A, D_CONV_A) * pcol(OFF_A, D_CONV_A)
        conv_a = vec(ROW_CONV_A + CONV_A_WIDTH - 1, D_CONV_A) * ua_ref[lo:lo + L, :]
        for k in range(1, CONV_A_WIDTH):
            conv_a += vec(ROW_CONV_A + CONV_A_WIDTH - 1 - k, D_CONV_A) * ua_ref[lo - k:lo - k + L, :]
        ycat_ref[rows, 0:D_CONV_A] = (pcol(OFF_A + D_CONV_A, D_CONV_A) * conv_a
                                      * _silu(pcol(OFF_A + 3 * D_CONV_A, D_CONV_A))).astype(BF16)
        yield

        xbc_ref[lo:lo + L, :] = pcol(OFF_XBC, SSD_CONV_DIM)
        conv_b = vec(ROW_SSD_CONV + SSD_CONV_WIDTH - 1, SSD_CONV_DIM) * xbc_ref[lo:lo + L, :]
        for k in range(1, SSD_CONV_WIDTH):
            conv_b += (vec(ROW_SSD_CONV + SSD_CONV_WIDTH - 1 - k, SSD_CONV_DIM)
                       * xbc_ref[lo - k:lo - k + L, :])
        xbc = _silu(conv_b + vec(ROW_SSD_CONV_B, SSD_CONV_DIM))
        xs = xbc[:, 0:D_SSD]
        bs = xbc[:, D_SSD:D_SSD + SSD_BC]
        cs = xbc[:, D_SSD + SSD_BC:SSD_CONV_DIM]
        yield
        dt_slab = _softplus(jnp.where(head_lane, pcol(OFF_SDT, LANES) + vec(ROW_DT_BIAS, LANES), 0.0))
        cum_slab = _dot_pieces(tril, _split3(dt_slab * a_neg2))
        yield
        dt_c = _dot_pieces(_split3(dt_slab)[:2], head_expand)
        cum = _dot_pieces(_split3(cum_slab), head_expand)
        tot = cum[L - 1:L, :]
        xd = xs * dt_c
        xd_b = xd.astype(BF16)
        bs_b = bs.astype(BF16)
        cs_b = cs.astype(BF16)
        state_b = state_ref[...].astype(BF16)

        cb = [_dot_nt(cs_b[:, g * SSD_STATE:(g + 1) * SSD_STATE],
                      bs_b[:, g * SSD_STATE:(g + 1) * SSD_STATE]) for g in range(SSD_GROUPS)]
        yield
        y_diag_slabs = []
        for s in range(D_SSD // LANES):
            cum_slab = cum[:, s * LANES:(s + 1) * LANES]
            cum_rolled = pltpu.roll(cum_slab, SSD_HEAD_DIM, 1)
            cum_t = cum_slab.T
            pair = []
            for e in range(2):
                hd = 2 * s + e
                col = jnp.where(lane_lo, cum_slab, cum_rolled) if e == 0 else jnp.where(lane_lo, cum_rolled, cum_slab)
                row = cum_t[e * SSD_HEAD_DIM:e * SSD_HEAD_DIM + 1, :]
                decay = jnp.exp2(jnp.where(causal, col - row, -jnp.inf))
                scores = (cb[hd // (SSD_HEADS // SSD_GROUPS)] * decay).astype(BF16)
                pair.append(_dot(scores, xd_b[:, s * LANES:(s + 1) * LANES]))
            y_diag_slabs.append(jnp.where(lane_lo, pair[0], pair[1]))
            yield
        y_diag = jnp.concatenate(y_diag_slabs, axis=1)

        y_off_g = [_dot(cs_b[:, g * SSD_STATE:(g + 1) * SSD_STATE], state_b) for g in range(SSD_GROUPS)]
        y_off = jnp.where(grp0, y_off_g[0], y_off_g[1]) * jnp.exp2(cum)

        xdd = (xd * jnp.exp2(tot - cum)).astype(BF16)
        add_g = [_dot(bs[:, g * SSD_STATE:(g + 1) * SSD_STATE].T.astype(BF16), xdd) for g in range(SSD_GROUPS)]
        state_ref[...] = state_ref[...] * jnp.exp2(tot) + jnp.where(grp0, add_g[0], add_g[1])
        yield

        gated = (y_diag + y_off + xs * vec(ROW_D_SKIP, D_SSD)) * _silu(pcol(OFF_SZ, D_SSD))
        g2 = gated * gated
        ss_all = jnp.sum(g2, axis=-1, keepdims=True)
        ss0 = jnp.sum(jnp.where(grp0, g2, 0.0), axis=-1, keepdims=True)
        inv0 = lax.rsqrt(ss0 * (1.0 / half) + SSD_NORM_EPS)
        inv1 = lax.rsqrt((ss_all - ss0) * (1.0 / half) + SSD_NORM_EPS)
        y_b = gated * jnp.where(grp0, inv0, inv1) * vec(ROW_SSD_NORM_G, D_SSD)
        ycat_ref[rows, D_CONV_A:D_CONV_A + D_SSD] = y_b.astype(BF16)
        yield

    scale = (QK_NOPE + QK_ROPE) ** -0.5 * LOG2E
    r0 = ROPE_ROW0
    kb = KEY_BLOCK

    def attend_prepare():
        ang = pos_ref[...].astype(F32) * invf_ref[...]
        cos_t = jnp.cos(ang)
        sin_t = jnp.sin(ang)

        def rope_rows(t1, t2):
            return t1 * cos_t - t2 * sin_t, t2 * cos_t + t1 * sin_t

        c_qa = ptile(OFF_QA, Q_LORA)
        qan = (c_qa * lax.rsqrt(jnp.mean(c_qa * c_qa, axis=-1, keepdims=True) + NORM_EPS)
               * vec(ROW_Q_NORM_G, Q_LORA)).astype(BF16)
        q_t = _dot_nt(wqbt_ref[...], qan)
        yield
        for hd in range(MLA_HEADS):
            base = hd * HEAD_SLAB
            q1, q2 = rope_rows(q_t[base + r0:base + r0 + ROPE_HALF, :],
                               q_t[base + r0 + ROPE_HALF:base + r0 + QK_ROPE, :])
            slab = jnp.concatenate([q_t[base:base + r0, :], q1, q2,
                                    q_t[base + r0 + QK_ROPE:base + HEAD_SLAB, :]], axis=0)
            qt_ref[base:base + HEAD_SLAB, :] = (slab * scale).astype(BF16)
        yield

        c_kv = ptile(OFF_KV, KV_LORA)
        kvn = (c_kv * lax.rsqrt(jnp.mean(c_kv * c_kv, axis=-1, keepdims=True) + NORM_EPS)
               * vec(ROW_KV_NORM_G, KV_LORA)).astype(BF16)
        kr_slab0 = OFF_KR // LANES * LANES
        kr_lane = lax.broadcasted_iota(jnp.int32, (tile, LANES), 1)
        kr_rolled = pltpu.roll(ptile(kr_slab0, LANES), r0 - (OFF_KR - kr_slab0), 1)
        ckr_t = jnp.where((kr_lane >= r0) & (kr_lane < r0 + QK_ROPE), kr_rolled, 0.0).T
        k1, k2 = rope_rows(ckr_t[r0:r0 + ROPE_HALF, :], ckr_t[r0 + ROPE_HALF:r0 + QK_ROPE, :])
        kr_nat = jnp.concatenate([ckr_t[0:r0, :], k1, k2, ckr_t[r0 + QK_ROPE:HEAD_SLAB, :]], axis=0).T
        yield
        k_nat = _dot(kvn, wk_ref[...])
        v_t = _dot_nt(wvt_ref[...], kvn)
        ones_row = lax.broadcasted_iota(jnp.int32, (V_SLAB, tile), 0) == ONES_ROW
        for hd in range(MLA_HEADS):
            base = hd * HEAD_SLAB
            kc_ref[j, :, base:base + HEAD_SLAB] = (k_nat[:, base:base + HEAD_SLAB] + kr_nat).astype(BF16)
            vt_ref[j, hd * V_SLAB:(hd + 1) * V_SLAB, :] = jnp.where(
                ones_row, 1.0, v_t[hd * V_SLAB:(hd + 1) * V_SLAB, :]).astype(BF16)
        m_ref[...] = jnp.full(m_ref.shape, -jnp.inf, F32)
        acc_ref[...] = jnp.zeros(acc_ref.shape, F32)
        yield

    def attend(i, masked):
        def span(sub):
            return (sub * kb if masked else 0), tile

        def scores(sub, hd):
            base = hd * HEAD_SLAB
            q0, q1 = span(sub)
            return _dot(kc_ref[i, sub * kb:(sub + 1) * kb, base:base + HEAD_SLAB],
                        qt_ref[base:base + HEAD_SLAB, q0:q1])

        steps = [(sub, hd) for sub in range(tile // kb) for hd in range(MLA_HEADS)]
        pending = [scores(*st) for st in steps[:SCORE_LOOKAHEAD]]
        for n, (sub, hd) in enumerate(steps):
            s_t = pending.pop(0)
            if n + SCORE_LOOKAHEAD < len(steps):
                pending.append(scores(*steps[n + SCORE_LOOKAHEAD]))
            vrows = slice(hd * V_SLAB, (hd + 1) * V_SLAB)
            q0, q1 = span(sub)
            if masked:
                visible = (lax.broadcasted_iota(jnp.int32, s_t.shape, 0)
                           <= lax.broadcasted_iota(jnp.int32, s_t.shape, 1))
                s_t = jnp.where(visible, s_t, -jnp.inf)
            m_old = m_ref[hd, :, q0:q1]
            m_new = jnp.maximum(m_old, jnp.max(s_t, axis=0, keepdims=True))
            p = jnp.exp2(s_t - m_new).astype(BF16)
            alpha = jnp.exp2(m_old - m_new)
            acc_ref[vrows, q0:q1] = (alpha * acc_ref[vrows, q0:q1]
                                     + _dot(vt_ref[i, vrows, sub * kb:(sub + 1) * kb], p))
            m_ref[hd, :, q0:q1] = m_new
            yield

    def attend_diagonal():
        yield from attend_prepare()
        yield from attend(j, True)

    d_ab = D_CONV_A + D_SSD

    def out_partial(chunks):
        for c in chunks:
            rows = slice(c * L, (c + 1) * L)
            o_ref[rows, :] = x_ref[rows, :] + _dot(ycat_ref[rows, 0:d_ab], w_out_ref[0:d_ab, :])
            yield

    _interleave(project(0))
    for c in range(n_chunks - 1):
        _interleave(project(c + 1), mix(c))
    _interleave((attend_diagonal(), DIAG_STAGES_PER_MIX_STAGE), mix(n_chunks - 1),
                out_partial(range(n_chunks - 1)))
    _interleave(out_partial([n_chunks - 1]))
    ua_ref[0:HALO, :] = ua_ref[tile:tile + HALO, :]
    xbc_ref[0:HALO, :] = xbc_ref[tile:tile + HALO, :]

    def body(i, carry):
        _interleave(attend(i, False))
        return carry

    lax.fori_loop(0, j, body, 0)

    for hd in range(MLA_HEADS):
        base = hd * V_SLAB
        yt_ref[hd * V_DIM:(hd + 1) * V_DIM, :] = (acc_ref[base:base + V_DIM, :]
                                                  / acc_ref[base + ONES_ROW:base + ONES_ROW + 1, :])
    c_z = ptile(OFF_CZ, D_MLA)
    ycat_ref[:, D_CONV_A + D_SSD:D_MIX] = (yt_ref[...].T * _silu(c_z)).astype(BF16)

    out = o_ref[...] + _dot(ycat_ref[:, d_ab:D_MIX], w_out_ref[d_ab:D_MIX, :])
    if final:
        out = (out * lax.rsqrt(jnp.mean(out * out, axis=-1, keepdims=True) + NORM_EPS)
               * vec(ROW_FINAL_G, D_MODEL))
    o_ref[...] = out


def _convert_pad_kernel(w_ref, o_ref):
    cols = w_ref.shape[-1]
    o_ref[:, 0:cols] = w_ref[...].astype(BF16)
    if o_ref.shape[-1] > cols:
        o_ref[:, cols:] = jnp.zeros((o_ref.shape[0], o_ref.shape[-1] - cols), BF16)


def _to_bf16_padded(w, width):
    depth, rows, cols = w.shape
    return pl.pallas_call(
        _convert_pad_kernel,
        out_shape=jax.ShapeDtypeStruct((depth, rows, width), BF16),
        grid=(depth, rows // CONVERT_ROWS),
        in_specs=[pl.BlockSpec((None, CONVERT_ROWS, cols), lambda l, r: (l, r, 0))],
        out_specs=pl.BlockSpec((None, CONVERT_ROWS, width), lambda l, r: (l, r, 0)),
        name="weights_to_bf16",
    )(w)


def _prep_layer(norm_g, conv_a_w, ssd_conv_w, ssd_conv_b, ssd_dt_bias, ssd_a_log, ssd_d,
                ssd_norm_g, mla_q_norm_g, w_qb, mla_kv_norm_g, w_kvb, final_norm_g):
    wqb = w_qb.reshape(Q_LORA, MLA_HEADS, QK_NOPE + QK_ROPE)
    wqbt = jnp.pad(wqb, ((0, 0), (0, 0), (0, HEAD_SLAB - QK_NOPE - QK_ROPE))).reshape(
        Q_LORA, MLA_HEADS * HEAD_SLAB).T.astype(BF16)
    wkv = w_kvb.reshape(KV_LORA, MLA_HEADS, QK_NOPE + V_DIM)
    wk = jnp.pad(wkv[:, :, :QK_NOPE], ((0, 0), (0, 0), (0, HEAD_SLAB - QK_NOPE))).reshape(
        KV_LORA, MLA_HEADS * HEAD_SLAB).astype(BF16)
    wvt = jnp.pad(wkv[:, :, QK_NOPE:], ((0, 0), (0, 0), (0, V_SLAB - V_DIM))).reshape(
        KV_LORA, MLA_HEADS * V_SLAB).T.astype(BF16)

    def row(v):
        return jnp.pad(v.astype(F32), (0, D_MODEL - v.shape[0]))[None, :]

    rows = [row(norm_g)]
    rows += [row(conv_a_w[k]) for k in range(CONV_A_WIDTH)]
    rows += [row(ssd_conv_w[k]) for k in range(SSD_CONV_WIDTH)]
    rows += [row(ssd_conv_b), row(ssd_dt_bias), row(ssd_a_log), row(jnp.repeat(ssd_d, SSD_HEAD_DIM)),
             row(ssd_norm_g), row(mla_q_norm_g), row(mla_kv_norm_g), row(final_norm_g)]
    rows += [jnp.zeros((N_VEC_ROWS - len(rows), D_MODEL), F32)]
    vecs = jnp.concatenate(rows, axis=0)
    return vecs, wqbt, wk, wvt


def _vmem_limit(tile, n_tiles):
    f32, bf16 = 4, 2
    blocks = 2 * (2 * tile * D_MODEL * f32)
    weights = 2 * bf16 * (D_MODEL * N_PROJ + Q_LORA * MLA_HEADS * HEAD_SLAB
                          + KV_LORA * MLA_HEADS * (HEAD_SLAB + V_SLAB) + D_MIX * D_MODEL)
    scratch = (tile * N_PROJ * f32 + (tile + HALO) * (D_CONV_A + SSD_CONV_DIM) * f32
               + SSD_STATE * D_SSD * f32 + n_tiles * tile * MLA_HEADS * (HEAD_SLAB + V_SLAB) * bf16
               + MLA_HEADS * HEAD_SLAB * tile * bf16 + MLA_HEADS * V_SLAB * tile * f32
               + D_MLA * tile * f32 + tile * D_MIX * bf16)
    temporaries = 2 * tile * N_PROJ * f32
    return min(blocks + weights + scratch + temporaries, V7X_VMEM_BYTES - 8 * 1024 * 1024)


def _layer_call(x, pos3, invf, params, w_in_all, w_out_all, *, layer, final):
    batch, seq, d_model = x.shape
    tile = SEQ_TILE
    n_tiles = seq // tile
    vecs, wqbt, wk, wvt = params
    const = lambda shape: pl.BlockSpec(shape, lambda b, j: (0,) * len(shape))
    of_layer = lambda a: pl.BlockSpec((None,) + a.shape[1:], lambda b, j: (layer, 0, 0))
    return pl.pallas_call(
        functools.partial(_layer_kernel, final=final),
        out_shape=jax.ShapeDtypeStruct(x.shape, F32),
        grid=(batch, n_tiles),
        in_specs=[
            pl.BlockSpec((None, 1, tile), lambda b, j: (b, 0, j)),
            pl.BlockSpec((None, tile, d_model), lambda b, j: (b, j, 0)),
            const(invf.shape), const(vecs.shape), of_layer(w_in_all),
            const(wqbt.shape), const(wk.shape), const(wvt.shape), of_layer(w_out_all),
        ],
        out_specs=pl.BlockSpec((None, tile, d_model), lambda b, j: (b, j, 0)),
        scratch_shapes=[
            pltpu.VMEM((tile + HALO, D_CONV_A), F32),
            pltpu.VMEM((tile + HALO, SSD_CONV_DIM), F32),
            pltpu.VMEM((SSD_STATE, D_SSD), F32),
            pltpu.VMEM((n_tiles, tile, MLA_HEADS * HEAD_SLAB), BF16),
            pltpu.VMEM((n_tiles, MLA_HEADS * V_SLAB, tile), BF16),
            pltpu.VMEM((MLA_HEADS * HEAD_SLAB, tile), BF16),
            pltpu.VMEM((MLA_HEADS * V_SLAB, tile), F32),
            pltpu.VMEM((MLA_HEADS, 1, tile), F32),
            pltpu.VMEM((D_MLA, tile), F32),
            pltpu.VMEM((tile, D_MIX), BF16),
        ] + [pltpu.VMEM((SSD_CHUNK, N_PROJ), F32)] * (tile // SSD_CHUNK),
        compiler_params=pltpu.CompilerParams(
            dimension_semantics=("arbitrary", "arbitrary"),
            vmem_limit_bytes=_vmem_limit(tile, n_tiles)),
        name="hybrid_layer_final" if final else "hybrid_layer",
    )(pos3, x, invf, vecs, w_in_all, wqbt, wk, wvt, w_out_all)


def kernel(x, positions, norm_g, w_in, conv_a_w, ssd_conv_w, ssd_conv_b, ssd_dt_bias, ssd_a_log, ssd_d,
           ssd_norm_g, mla_q_norm_g, w_qb, mla_kv_norm_g, w_kvb, w_out, final_norm_g):
    batch, seq, _ = x.shape
    depth = norm_g.shape[0]
    assert seq % SEQ_TILE == 0 and SEQ_TILE % SSD_CHUNK == 0 and SEQ_TILE % KEY_BLOCK == 0
    inv_freq = ROPE_BASE ** (-jnp.arange(0, QK_ROPE, 2, dtype=F32) / QK_ROPE)
    invf = jnp.broadcast_to(inv_freq[:, None], (ROPE_HALF, SEQ_TILE))
    pos3 = positions.reshape(batch, 1, seq)
    w_in_all = _to_bf16_padded(w_in, N_PROJ)
    w_out_all = _to_bf16_padded(w_out, D_MODEL)
    for l in range(depth):
        params = _prep_layer(norm_g[l], conv_a_w[l], ssd_conv_w[l], ssd_conv_b[l], ssd_dt_bias[l],
                             ssd_a_log[l], ssd_d[l], ssd_norm_g[l], mla_q_norm_g[l], w_qb[l],
                             mla_kv_norm_g[l], w_kvb[l], final_norm_g)
        x = _layer_call(x, pos3, invf, params, w_in_all, w_out_all, layer=l, final=(l == depth - 1))
    return x
```

```python
import functools

import jax
import jax.numpy as jnp
from jax import lax
from jax.experimental import pallas as pl
from jax.experimental.pallas import tpu as pltpu

F32 = jnp.float32
BF16 = jnp.bfloat16

D_MODEL = 1024
D_CONV_A = 256
CONV_A_WIDTH = 3
SSD_HEADS = 6
SSD_HEAD_DIM = 64
D_SSD = SSD_HEADS * SSD_HEAD_DIM
SSD_GROUPS = 2
SSD_STATE = 128
SSD_CONV_WIDTH = 4
SSD_BC = SSD_GROUPS * SSD_STATE
SSD_CONV_DIM = D_SSD + 2 * SSD_BC
SSD_NORM_EPS = 1e-5
MLA_HEADS = 6
Q_LORA = 256
KV_LORA = 128
QK_NOPE = 64
QK_ROPE = 32
V_DIM = 64
D_MLA = MLA_HEADS * V_DIM
ROPE_BASE = 10000.0
D_MIX = D_CONV_A + D_SSD + D_MLA
NORM_EPS = 1e-6

LANES = 128
SUBLANES = 8
V7X_VMEM_BYTES = 64 * 1024 * 1024

SEQ_TILE = 512
SSD_CHUNK = 128
HEAD_SLAB = LANES
ROPE_HALF = QK_ROPE // 2
ROPE_ROW0 = QK_NOPE
ONES_ROW = V_DIM
V_SLAB = 80
KEY_BLOCK = 128
SCORE_LOOKAHEAD = 4
PROJ_GROUP = 512
DIAG_STAGES_PER_MIX_STAGE = 3
LOG2E = 1.4426950408889634
HALO = SUBLANES

OFF_A = 0
OFF_SZ = OFF_A + 4 * D_CONV_A
OFF_XBC = OFF_SZ + D_SSD
OFF_SDT = OFF_XBC + SSD_CONV_DIM
OFF_QA = OFF_SDT + SSD_HEADS
OFF_KV = OFF_QA + Q_LORA
OFF_KR = OFF_KV + KV_LORA
OFF_CZ = OFF_KR + QK_ROPE
IN_COLS = OFF_CZ + D_MLA
N_PROJ = -(-IN_COLS // LANES) * LANES

ROW_NORM_G = 0
ROW_CONV_A = 1
ROW_SSD_CONV = ROW_CONV_A + CONV_A_WIDTH
ROW_SSD_CONV_B = ROW_SSD_CONV + SSD_CONV_WIDTH
ROW_DT_BIAS = ROW_SSD_CONV_B + 1
ROW_A_LOG = ROW_DT_BIAS + 1
ROW_D_SKIP = ROW_A_LOG + 1
ROW_SSD_NORM_G = ROW_D_SKIP + 1
ROW_Q_NORM_G = ROW_SSD_NORM_G + 1
ROW_KV_NORM_G = ROW_Q_NORM_G + 1
ROW_FINAL_G = ROW_KV_NORM_G + 1
N_VEC_ROWS = 16


def _silu(v):
    return v * (1.0 / (1.0 + jnp.exp(-v)))


def _softplus(v):
    return jnp.maximum(v, 0.0) + jnp.log1p(jnp.exp(-jnp.abs(v)))


def _split3(v):
    hi = v.astype(BF16)
    r1 = v - hi.astype(F32)
    mid = r1.astype(BF16)
    lo = (r1 - mid.astype(F32)).astype(BF16)
    return hi, mid, lo


def _dot(a, b):
    return jnp.dot(a, b, preferred_element_type=F32)


def _dot_pieces(a, b):
    if isinstance(a, tuple):
        return functools.reduce(lambda acc, piece: acc + _dot(piece, b), a[1:], _dot(a[0], b))
    return functools.reduce(lambda acc, piece: acc + _dot(a, piece), b[1:], _dot(a, b[0]))


def _dot_nt(a, b):
    return lax.dot_general(a, b, (((1,), (1,)), ((), ())), preferred_element_type=F32)


def _interleave(*stage_iters):
    live = [it if isinstance(it, tuple) else (it, 1) for it in stage_iters]
    while live:
        for entry in list(live):
            it, per_round = entry
            for _ in range(per_round):
                if next(it, StopIteration) is StopIteration:
                    live.remove(entry)
                    break


def _layer_kernel(pos_ref, x_ref, invf_ref, vec_ref, w_in_ref, wqbt_ref, wk_ref,
                  wvt_ref, w_out_ref, o_ref,
                  ua_ref, xbc_ref, state_ref, kc_ref, vt_ref, qt_ref, acc_ref,
                  m_ref, yt_ref, ycat_ref, *proj_refs, final):
    j = pl.program_id(1)
    tile = x_ref.shape[0]

    def ptile(off, width):
        return jnp.concatenate([p[:, off:off + width] for p in proj_refs], axis=0)

    def vec(row, width):
        return vec_ref[row:row + 1, 0:width]

    L = SSD_CHUNK
    n_chunks = tile // L

    @pl.when(j == 0)
    def _():
        ua_ref[0:HALO, :] = jnp.zeros((HALO, D_CONV_A), F32)
        xbc_ref[0:HALO, :] = jnp.zeros((HALO, SSD_CONV_DIM), F32)
        state_ref[...] = jnp.zeros(state_ref.shape, F32)

    row_i = lax.broadcasted_iota(jnp.int32, (L, L), 0)
    col_i = lax.broadcasted_iota(jnp.int32, (L, L), 1)
    causal = row_i >= col_i
    tril = causal.astype(BF16)
    lane_lo = lax.broadcasted_iota(jnp.int32, (L, LANES), 1) < SSD_HEAD_DIM
    half = D_SSD // SSD_GROUPS
    grp0 = lax.broadcasted_iota(jnp.int32, (1, D_SSD), 1) < half
    head_lane = lax.broadcasted_iota(jnp.int32, (1, LANES), 1) < SSD_HEADS
    a_neg2 = -jnp.exp(vec(ROW_A_LOG, LANES)) * LOG2E
    head_of_lane = lax.shift_right_logical(lax.broadcasted_iota(jnp.int32, (LANES, D_SSD), 1),
                                           jnp.int32(SSD_HEAD_DIM.bit_length() - 1))
    head_expand = (lax.broadcasted_iota(jnp.int32, (LANES, D_SSD), 0) == head_of_lane).astype(BF16)

    def project(c):
        rows = slice(c * L, (c + 1) * L)
        x = x_ref[rows, :]
        h = x * lax.rsqrt(jnp.mean(x * x, axis=-1, keepdims=True) + NORM_EPS) * vec(ROW_NORM_G, D_MODEL)
        hb = h.astype(BF16)
        yield
        for g0 in range(0, N_PROJ, PROJ_GROUP):
            g1 = min(g0 + PROJ_GROUP, N_PROJ)
            proj_refs[c][:, g0:g1] = _dot(hb, w_in_ref[:, g0:g1])
            yield

    def mix(c):
        rows = slice(c * L, (c + 1) * L)
        lo = HALO + c * L

        def pcol(off, width):
            return proj_refs[c][:, off:off + width]

        ua_ref[lo:lo + L, :] = pcol(OFF_A + 2 * D_CONV_A, D_CONV_A) * pcol(OFF_A, D_CONV_A)
        conv_a = vec(ROW_CONV_A + CONV_A_WIDTH - 1, D_CONV_A) * ua_ref[lo:lo + L, :]
        for k in range(1, CONV_A_WIDTH):
            conv_a += vec(ROW_CONV_A + CONV_A_WIDTH - 1 - k, D_CONV_A) * ua_ref[lo - k:lo - k + L, :]
        ycat_ref[rows, 0:D_CONV_A] = (pcol(OFF_A + D_CONV_A, D_CONV_A) * conv_a
                                      * _silu(pcol(OFF_A + 3 * D_CONV_A, D_CONV_A))).astype(BF16)
        yield

        xbc_ref[lo:lo + L, :] = pcol(OFF_XBC, SSD_CONV_DIM)
        conv_b = vec(ROW_SSD_CONV + SSD_CONV_WIDTH - 1, SSD_CONV_DIM) * xbc_ref[lo:lo + L, :]
        for k in range(1, SSD_CONV_WIDTH):
            conv_b += (vec(ROW_SSD_CONV + SSD_CONV_WIDTH - 1 - k, SSD_CONV_DIM)
                       * xbc_ref[lo - k:lo - k + L, :])
        xbc = _silu(conv_b + vec(ROW_SSD_CONV_B, SSD_CONV_DIM))
        xs = xbc[:, 0:D_SSD]
        bs = xbc[:, D_SSD:D_SSD + SSD_BC]
        cs = xbc[:, D_SSD + SSD_BC:SSD_CONV_DIM]
        yield
        dt_slab = _softplus(jnp.where(head_lane, pcol(OFF_SDT, LANES) + vec(ROW_DT_BIAS, LANES), 0.0))
        cum_slab = _dot_pieces(tril, _split3(dt_slab * a_neg2))
        yield
        dt_c = _dot_pieces(_split3(dt_slab)[:2], head_expand)
        cum = _dot_pieces(_split3(cum_slab), head_expand)
        tot = cum[L - 1:L, :]
        xd = xs * dt_c
        xd_b = xd.astype(BF16)
        bs_b = bs.astype(BF16)
        cs_b = cs.astype(BF16)
        state_b = state_ref[...].astype(BF16)

        cb = [_dot_nt(cs_b[:, g * SSD_STATE:(g + 1) * SSD_STATE],
                      bs_b[:, g * SSD_STATE:(g + 1) * SSD_STATE]) for g in range(SSD_GROUPS)]
        yield
        y_diag_slabs = []
        for s in range(D_SSD // LANES):
            cum_slab = cum[:, s * LANES:(s + 1) * LANES]
            cum_rolled = pltpu.roll(cum_slab, SSD_HEAD_DIM, 1)
            cum_t = cum_slab.T
            pair = []
            for e in range(2):
                hd = 2 * s + e
                col = jnp.where(lane_lo, cum_slab, cum_rolled) if e == 0 else jnp.where(lane_lo, cum_rolled, cum_slab)
                row = cum_t[e * SSD_HEAD_DIM:e * SSD_HEAD_DIM + 1, :]
                decay = jnp.exp2(jnp.where(causal, col - row, -jnp.inf))
                scores = (cb[hd // (SSD_HEADS // SSD_GROUPS)] * decay).astype(BF16)
                pair.append(_dot(scores, xd_b[:, s * LANES:(s + 1) * LANES]))
            y_diag_slabs.append(jnp.where(lane_lo, pair[0], pair[1]))
            yield
        y_diag = jnp.concatenate(y_diag_slabs, axis=1)

        y_off_g = [_dot(cs_b[:, g * SSD_STATE:(g + 1) * SSD_STATE], state_b) for g in range(SSD_GROUPS)]
        y_off = jnp.where(grp0, y_off_g[0], y_off_g[1]) * jnp.exp2(cum)

        xdd = (xd * jnp.exp2(tot - cum)).astype(BF16)
        add_g = [_dot(bs[:, g * SSD_STATE:(g + 1) * SSD_STATE].T.astype(BF16), xdd) for g in range(SSD_GROUPS)]
        state_ref[...] = state_ref[...] * jnp.exp2(tot) + jnp.where(grp0, add_g[0], add_g[1])
        yield

        gated = (y_diag + y_off + xs * vec(ROW_D_SKIP, D_SSD)) * _silu(pcol(OFF_SZ, D_SSD))
        g2 = gated * gated
        ss_all = jnp.sum(g2, axis=-1, keepdims=True)
        ss0 = jnp.sum(jnp.where(grp0, g2, 0.0), axis=-1, keepdims=True)
        inv0 = lax.rsqrt(ss0 * (1.0 / half) + SSD_NORM_EPS)
        inv1 = lax.rsqrt((ss_all - ss0) * (1.0 / half) + SSD_NORM_EPS)
        y_b = gated * jnp.where(grp0, inv0, inv1) * vec(ROW_SSD_NORM_G, D_SSD)
        ycat_ref[rows, D_CONV_A:D_CONV_A + D_SSD] = y_b.astype(BF16)
        yield

    scale = (QK_NOPE + QK_ROPE) ** -0.5 * LOG2E
    r0 = ROPE_ROW0
    kb = KEY_BLOCK

    def attend_prepare():
        ang = pos_ref[...].astype(F32) * invf_ref[...]
        cos_t = jnp.cos(ang)
        sin_t = jnp.sin(ang)

        def rope_rows(t1, t2):
            return t1 * cos_t - t2 * sin_t, t2 * cos_t + t1 * sin_t

        c_qa = ptile(OFF_QA, Q_LORA)
        qan = (c_qa * lax.rsqrt(jnp.mean(c_qa * c_qa, axis=-1, keepdims=True) + NORM_EPS)
               * vec(ROW_Q_NORM_G, Q_LORA)).astype(BF16)
        q_t = _dot_nt(wqbt_ref[...], qan)
        yield
        for hd in range(MLA_HEADS):
            base = hd * HEAD_SLAB
            q1, q2 = rope_rows(q_t[base + r0:base + r0 + ROPE_HALF, :],
                               q_t[base + r0 + ROPE_HALF:base + r0 + QK_ROPE, :])
            slab = jnp.concatenate([q_t[base:base + r0, :], q1, q2,
                                    q_t[base + r0 + QK_ROPE:base + HEAD_SLAB, :]], axis=0)
            qt_ref[base:base + HEAD_SLAB, :] = (slab * scale).astype(BF16)
        yield

        c_kv = ptile(OFF_KV, KV_LORA)
        kvn = (c_kv * lax.rsqrt(jnp.mean(c_kv * c_kv, axis=-1, keepdims=True) + NORM_EPS)
               * vec(ROW_KV_NORM_G, KV_LORA)).astype(BF16)
        kr_slab0 = OFF_KR // LANES * LANES
        kr_lane = lax.broadcasted_iota(jnp.int32, (tile, LANES), 1)
        kr_rolled = pltpu.roll(ptile(kr_slab0, LANES), r0 - (OFF_KR - kr_slab0), 1)
        ckr_t = jnp.where((kr_lane >= r0) & (kr_lane < r0 + QK_ROPE), kr_rolled, 0.0).T
        k1, k2 = rope_rows(ckr_t[r0:r0 + ROPE_HALF, :], ckr_t[r0 + ROPE_HALF:r0 + QK_ROPE, :])
        kr_nat = jnp.concatenate([ckr_t[0:r0, :], k1, k2, ckr_t[r0 + QK_ROPE:HEAD_SLAB, :]], axis=0).T
        yield
        k_nat = _dot(kvn, wk_ref[...])
        v_t = _dot_nt(wvt_ref[...], kvn)
        ones_row = lax.broadcasted_iota(jnp.int32, (V_SLAB, tile), 0) == ONES_ROW
        for hd in range(MLA_HEADS):
            base = hd * HEAD_SLAB
            kc_ref[j, :, base:base + HEAD_SLAB] = (k_nat[:, base:base + HEAD_SLAB] + kr_nat).astype(BF16)
            vt_ref[j, hd * V_SLAB:(hd + 1) * V_SLAB, :] = jnp.where(
                ones_row, 1.0, v_t[hd * V_SLAB:(hd + 1) * V_SLAB, :]).astype(BF16)
        m_ref[...] = jnp.full(m_ref.shape, -jnp.inf, F32)
        acc_ref[...] = jnp.zeros(acc_ref.shape, F32)
        yield

    def attend(i, masked):
        def span(sub):
            return (sub * kb if masked else 0), tile

        def scores(sub, hd):
            base = hd * HEAD_SLAB
            q0, q1 = span(sub)
            return _dot(kc_ref[i, sub * kb:(sub + 1) * kb, base:base + HEAD_SLAB],
                        qt_ref[base:base + HEAD_SLAB, q0:q1])

        steps = [(sub, hd) for sub in range(tile // kb) for hd in range(MLA_HEADS)]
        pending = [scores(*st) for st in steps[:SCORE_LOOKAHEAD]]
        for n, (sub, hd) in enumerate(steps):
            s_t = pending.pop(0)
            if n + SCORE_LOOKAHEAD < len(steps):
                pending.append(scores(*steps[n + SCORE_LOOKAHEAD]))
            vrows = slice(hd * V_SLAB, (hd + 1) * V_SLAB)
            q0, q1 = span(sub)
            if masked:
                visible = (lax.broadcasted_iota(jnp.int32, s_t.shape, 0)
                           <= lax.broadcasted_iota(jnp.int32, s_t.shape, 1))
                s_t = jnp.where(visible, s_t, -jnp.inf)
            m_old = m_ref[hd, :, q0:q1]
            m_new = jnp.maximum(m_old, jnp.max(s_t, axis=0, keepdims=True))
            p = jnp.exp2(s_t - m_new).astype(BF16)
            alpha = jnp.exp2(m_old - m_new)
            acc_ref[vrows, q0:q1] = (alpha * acc_ref[vrows, q0:q1]
                                     + _dot(vt_ref[i, vrows, sub * kb:(sub + 1) * kb], p))
            m_ref[hd, :, q0:q1] = m_new
            yield

    def attend_diagonal():
        yield from attend_prepare()
        yield from attend(j, True)

    d_ab = D_CONV_A + D_SSD

    def out_partial(chunks):
        for c in chunks:
            rows = slice(c * L, (c + 1) * L)
            o_ref[rows, :] = x_ref[rows, :] + _dot(ycat_ref[rows, 0:d_ab], w_out_ref[0:d_ab, :])
            yield

    _interleave(project(0))
    for c in range(n_chunks - 1):
        _interleave(project(c + 1), mix(c))
    _interleave((attend_diagonal(), DIAG_STAGES_PER_MIX_STAGE), mix(n_chunks - 1),
                out_partial(range(n_chunks - 1)))
    _interleave(out_partial([n_chunks - 1]))
    ua_ref[0:HALO, :] = ua_ref[tile:tile + HALO, :]
    xbc_ref[0:HALO, :] = xbc_ref[tile:tile + HALO, :]

    def body(i, carry):
        _interleave(attend(i, False))
        return carry

    lax.fori_loop(0, j, body, 0)

    for hd in range(MLA_HEADS):
        base = hd * V_SLAB
        yt_ref[hd * V_DIM:(hd + 1) * V_DIM, :] = (acc_ref[base:base + V_DIM, :]
                                                  / acc_ref[base + ONES_ROW:base + ONES_ROW + 1, :])
    c_z = ptile(OFF_CZ, D_MLA)
    ycat_ref[:, D_CONV_A + D_SSD:D_MIX] = (yt_ref[...].T * _silu(c_z)).astype(BF16)

    out = o_ref[...] + _dot(ycat_ref[:, d_ab:D_MIX], w_out_ref[d_ab:D_MIX, :])
    if final:
        out = (out * lax.rsqrt(jnp.mean(out * out, axis=-1, keepdims=True) + NORM_EPS)
               * vec(ROW_FINAL_G, D_MODEL))
    o_ref[...] = out


def _prep_layer(norm_g, conv_a_w, ssd_conv_w, ssd_conv_b, ssd_dt_bias, ssd_a_log, ssd_d,
                ssd_norm_g, mla_q_norm_g, w_qb, mla_kv_norm_g, w_kvb, final_norm_g):
    wqb = w_qb.reshape(Q_LORA, MLA_HEADS, QK_NOPE + QK_ROPE)
    wqbt = jnp.pad(wqb, ((0, 0), (0, 0), (0, HEAD_SLAB - QK_NOPE - QK_ROPE))).reshape(
        Q_LORA, MLA_HEADS * HEAD_SLAB).T.astype(BF16)
    wkv = w_kvb.reshape(KV_LORA, MLA_HEADS, QK_NOPE + V_DIM)
    wk = jnp.pad(wkv[:, :, :QK_NOPE], ((0, 0), (0, 0), (0, HEAD_SLAB - QK_NOPE))).reshape(
        KV_LORA, MLA_HEADS * HEAD_SLAB).astype(BF16)
    wvt = jnp.pad(wkv[:, :, QK_NOPE:], ((0, 0), (0, 0), (0, V_SLAB - V_DIM))).reshape(
        KV_LORA, MLA_HEADS * V_SLAB).T.astype(BF16)

    def row(v):
        return jnp.pad(v.astype(F32), (0, D_MODEL - v.shape[0]))[None, :]

    rows = [row(norm_g)]
    rows += [row(conv_a_w[k]) for k in range(CONV_A_WIDTH)]
    rows += [row(ssd_conv_w[k]) for k in range(SSD_CONV_WIDTH)]
    rows += [row(ssd_conv_b), row(ssd_dt_bias), row(ssd_a_log), row(jnp.repeat(ssd_d, SSD_HEAD_DIM)),
             row(ssd_norm_g), row(mla_q_norm_g), row(mla_kv_norm_g), row(final_norm_g)]
    rows += [jnp.zeros((N_VEC_ROWS - len(rows), D_MODEL), F32)]
    vecs = jnp.concatenate(rows, axis=0)
    return vecs, wqbt, wk, wvt


def _vmem_limit(tile, n_tiles):
    f32, bf16 = 4, 2
    blocks = 2 * (2 * tile * D_MODEL * f32)
    weights = 2 * bf16 * (D_MODEL * N_PROJ + Q_LORA * MLA_HEADS * HEAD_SLAB
                          + KV_LORA * MLA_HEADS * (HEAD_SLAB + V_SLAB) + D_MIX * D_MODEL)
    scratch = (tile * N_PROJ * f32 + (tile + HALO) * (D_CONV_A + SSD_CONV_DIM) * f32
               + SSD_STATE * D_SSD * f32 + n_tiles * tile * MLA_HEADS * (HEAD_SLAB + V_SLAB) * bf16
               + MLA_HEADS * HEAD_SLAB * tile * bf16 + MLA_HEADS * V_SLAB * tile * f32
               + D_MLA * tile * f32 + tile * D_MIX * bf16)
    temporaries = 2 * tile * N_PROJ * f32
    return min(blocks + weights + scratch + temporaries, V7X_VMEM_BYTES - 8 * 1024 * 1024)


def _layer_call(x, pos3, invf, params, w_in_all, w_out_all, *, layer, final):
    batch, seq, d_model = x.shape
    tile = SEQ_TILE
    n_tiles = seq // tile
    vecs, wqbt, wk, wvt = params
    const = lambda shape: pl.BlockSpec(shape, lambda b, j: (0,) * len(shape))
    of_layer = lambda a: pl.BlockSpec((None,) + a.shape[1:], lambda b, j: (layer, 0, 0))
    return pl.pallas_call(
        functools.partial(_layer_kernel, final=final),
        out_shape=jax.ShapeDtypeStruct(x.shape, F32),
        grid=(batch, n_tiles),
        in_specs=[
            pl.BlockSpec((None, 1, tile), lambda b, j: (b, 0, j)),
            pl.BlockSpec((None, tile, d_model), lambda b, j: (b, j, 0)),
            const(invf.shape), const(vecs.shape), of_layer(w_in_all),
            const(wqbt.shape), const(wk.shape), const(wvt.shape), of_layer(w_out_all),
        ],
        out_specs=pl.BlockSpec((None, tile, d_model), lambda b, j: (b, j, 0)),
        scratch_shapes=[
            pltpu.VMEM((tile + HALO, D_CONV_A), F32),
            pltpu.VMEM((tile + HALO, SSD_CONV_DIM), F32),
            pltpu.VMEM((SSD_STATE, D_SSD), F32),
            pltpu.VMEM((n_tiles, tile, MLA_HEADS * HEAD_SLAB), BF16),
            pltpu.VMEM((n_tiles, MLA_HEADS * V_SLAB, tile), BF16),
            pltpu.VMEM((MLA_HEADS * HEAD_SLAB, tile), BF16),
            pltpu.VMEM((MLA_HEADS * V_SLAB, tile), F32),
            pltpu.VMEM((MLA_HEADS, 1, tile), F32),
            pltpu.VMEM((D_MLA, tile), F32),
            pltpu.VMEM((tile, D_MIX), BF16),
        ] + [pltpu.VMEM((SSD_CHUNK, N_PROJ), F32)] * (tile // SSD_CHUNK),
        compiler_params=pltpu.CompilerParams(
            dimension_semantics=("arbitrary", "arbitrary"),
            vmem_limit_bytes=_vmem_limit(tile, n_tiles)),
        name="hybrid_layer_final" if final else "hybrid_layer",
    )(pos3, x, invf, vecs, w_in_all, wqbt, wk, wvt, w_out_all)


def kernel(x, positions, norm_g, w_in, conv_a_w, ssd_conv_w, ssd_conv_b, ssd_dt_bias, ssd_a_log, ssd_d,
           ssd_norm_g, mla_q_norm_g, w_qb, mla_kv_norm_g, w_kvb, w_out, final_norm_g):
    batch, seq, _ = x.shape
    depth = norm_g.shape[0]
    assert seq % SEQ_TILE == 0 and SEQ_TILE % SSD_CHUNK == 0 and SEQ_TILE % KEY_BLOCK == 0
    inv_freq = ROPE_BASE ** (-jnp.arange(0, QK_ROPE, 2, dtype=F32) / QK_ROPE)
    invf = jnp.broadcast_to(inv_freq[:, None], (ROPE_HALF, SEQ_TILE))
    pos3 = positions.reshape(batch, 1, seq)
    w_in_all = jnp.pad(w_in.astype(BF16), ((0, 0), (0, 0), (0, N_PROJ - IN_COLS)))
    w_out_all = w_out.astype(BF16)
    for l in range(depth):
        params = _prep_layer(norm_g[l], conv_a_w[l], ssd_conv_w[l], ssd_conv_b[l], ssd_dt_bias[l],
                             ssd_a_log[l], ssd_d[l], ssd_norm_g[l], mla_q_norm_g[l], w_qb[l],
                             mla_kv_norm_g[l], w_kvb[l], final_norm_g)
        x = _layer_call(x, pos3, invf, params, w_in_all, w_out_all, layer=l, final=(l == depth - 1))
    return x
```

```python
import functools

import jax
import jax.numpy as jnp
from jax import lax
from jax.experimental import pallas as pl
from jax.experimental.pallas import tpu as pltpu

F32 = jnp.float32
BF16 = jnp.bfloat16

D_MODEL = 1024
D_CONV_A = 256
CONV_A_WIDTH = 3
SSD_HEADS = 6
SSD_HEAD_DIM = 64
D_SSD = SSD_HEADS * SSD_HEAD_DIM
SSD_GROUPS = 2
SSD_STATE = 128
SSD_CONV_WIDTH = 4
SSD_BC = SSD_GROUPS * SSD_STATE
SSD_CONV_DIM = D_SSD + 2 * SSD_BC
SSD_NORM_EPS = 1e-5
MLA_HEADS = 6
Q_LORA = 256
KV_LORA = 128
QK_NOPE = 64
QK_ROPE = 32
V_DIM = 64
D_MLA = MLA_HEADS * V_DIM
ROPE_BASE = 10000.0
D_MIX = D_CONV_A + D_SSD + D_MLA
NORM_EPS = 1e-6

LANES = 128
SUBLANES = 8
V7X_VMEM_BYTES = 64 * 1024 * 1024

SEQ_TILE = 512
SSD_CHUNK = 128
HEAD_SLAB = LANES
ROPE_HALF = QK_ROPE // 2
ROPE_ROW0 = QK_NOPE
ONES_ROW = V_DIM
V_SLAB = 80
KEY_BLOCK = 128
QUERY_BLOCK = 256
DIAG_LOOKAHEAD = 8
LOOP_LOOKAHEAD = 12
PROJ_GROUP = 512
DIAG_STAGES_PER_MIX_STAGE = 3
LOG2E = 1.4426950408889634
HALO = SUBLANES

OFF_A = 0
OFF_SZ = OFF_A + 4 * D_CONV_A
OFF_XBC = OFF_SZ + D_SSD
OFF_SDT = OFF_XBC + SSD_CONV_DIM
OFF_QA = OFF_SDT + SSD_HEADS
OFF_KV = OFF_QA + Q_LORA
OFF_KR = OFF_KV + KV_LORA
OFF_CZ = OFF_KR + QK_ROPE
IN_COLS = OFF_CZ + D_MLA
N_PROJ = -(-IN_COLS // LANES) * LANES

ROW_NORM_G = 0
ROW_CONV_A = 1
ROW_SSD_CONV = ROW_CONV_A + CONV_A_WIDTH
ROW_SSD_CONV_B = ROW_SSD_CONV + SSD_CONV_WIDTH
ROW_DT_BIAS = ROW_SSD_CONV_B + 1
ROW_A_LOG = ROW_DT_BIAS + 1
ROW_D_SKIP = ROW_A_LOG + 1
ROW_SSD_NORM_G = ROW_D_SKIP + 1
ROW_Q_NORM_G = ROW_SSD_NORM_G + 1
ROW_KV_NORM_G = ROW_Q_NORM_G + 1
ROW_FINAL_G = ROW_KV_NORM_G + 1
N_VEC_ROWS = 16


def _silu(v):
    return v * (1.0 / (1.0 + jnp.exp(-v)))


def _softplus(v):
    return jnp.maximum(v, 0.0) + jnp.log1p(jnp.exp(-jnp.abs(v)))


def _split3(v):
    hi = v.astype(BF16)
    r1 = v - hi.astype(F32)
    mid = r1.astype(BF16)
    lo = (r1 - mid.astype(F32)).astype(BF16)
    return hi, mid, lo


def _dot(a, b):
    return jnp.dot(a, b, preferred_element_type=F32)


def _dot_pieces(a, b):
    if isinstance(a, tuple):
        return functools.reduce(lambda acc, piece: acc + _dot(piece, b), a[1:], _dot(a[0], b))
    return functools.reduce(lambda acc, piece: acc + _dot(a, piece), b[1:], _dot(a, b[0]))


def _dot_nt(a, b):
    return lax.dot_general(a, b, (((1,), (1,)), ((), ())), preferred_element_type=F32)


def _interleave(*stage_iters):
    live = [it if isinstance(it, tuple) else (it, 1) for it in stage_iters]
    while live:
        for entry in list(live):
            it, per_round = entry
            for _ in range(per_round):
                if next(it, StopIteration) is StopIteration:
                    live.remove(entry)
                    break


def _layer_kernel(pos_ref, x_ref, invf_ref, vec_ref, w_in_ref, wqbt_ref, wk_ref,
                  wvt_ref, w_out_ref, o_ref,
                  ua_ref, xbc_ref, state_ref, kc_ref, vt_ref, qt_ref, acc_ref,
                  m_ref, yt_ref, ycat_ref, tril_ref, expand_ref, *proj_refs, final):
    j = pl.program_id(1)
    tile = x_ref.shape[0]

    def ptile(off, width):
        return jnp.concatenate([p[:, off:off + width] for p in proj_refs], axis=0)

    def vec(row, width):
        return vec_ref[row:row + 1, 0:width]

    L = SSD_CHUNK
    n_chunks = tile // L

    @pl.when(j == 0)
    def _():
        ua_ref[0:HALO, :] = jnp.zeros((HALO, D_CONV_A), F32)
        xbc_ref[0:HALO, :] = jnp.zeros((HALO, SSD_CONV_DIM), F32)
        state_ref[...] = jnp.zeros(state_ref.shape, F32)

    row_i = lax.broadcasted_iota(jnp.int32, (L, L), 0)
    col_i = lax.broadcasted_iota(jnp.int32, (L, L), 1)
    causal = row_i >= col_i
    tril_ref[...] = jnp.where(causal, 1.0, 0.0).astype(BF16)
    lane_lo = lax.broadcasted_iota(jnp.int32, (L, LANES), 1) < SSD_HEAD_DIM
    half = D_SSD // SSD_GROUPS
    grp0 = lax.broadcasted_iota(jnp.int32, (1, D_SSD), 1) < half
    grp0_state = lax.broadcasted_iota(jnp.int32, (1, SSD_BC), 1) < SSD_STATE
    head_lane = lax.broadcasted_iota(jnp.int32, (1, LANES), 1) < SSD_HEADS
    a_neg2 = -jnp.exp(vec(ROW_A_LOG, LANES)) * LOG2E
    head_of_lane = lax.shift_right_logical(lax.broadcasted_iota(jnp.int32, (LANES, D_SSD), 1),
                                           jnp.int32(SSD_HEAD_DIM.bit_length() - 1))
    expand_ref[...] = jnp.where(lax.broadcasted_iota(jnp.int32, (LANES, D_SSD), 0) == head_of_lane,
                                1.0, 0.0).astype(BF16)

    def project(c):
        rows = slice(c * L, (c + 1) * L)
        x = x_ref[rows, :]
        h = x * lax.rsqrt(jnp.mean(x * x, axis=-1, keepdims=True) + NORM_EPS) * vec(ROW_NORM_G, D_MODEL)
        hb = h.astype(BF16)
        yield
        for g0 in range(0, N_PROJ, PROJ_GROUP):
            g1 = min(g0 + PROJ_GROUP, N_PROJ)
            proj_refs[c][:, g0:g1] = _dot(hb, w_in_ref[:, g0:g1])
            yield

    def mix(c):
        rows = slice(c * L, (c + 1) * L)
        lo = HALO + c * L

        def pcol(off, width):
            return proj_refs[c][:, off:off + width]

        ua_ref[lo:lo + L, :] = pcol(OFF_A + 2 * D_CONV_A, D_CONV_A) * pcol(OFF_A, D_CONV_A)
        conv_a = vec(ROW_CONV_A + CONV_A_WIDTH - 1, D_CONV_A) * ua_ref[lo:lo + L, :]
        for k in range(1, CONV_A_WIDTH):
            conv_a += vec(ROW_CONV_A + CONV_A_WIDTH - 1 - k, D_CONV_A) * ua_ref[lo - k:lo - k + L, :]
        ycat_ref[rows, 0:D_CONV_A] = (pcol(OFF_A + D_CONV_A, D_CONV_A) * conv_a
                                      * _silu(pcol(OFF_A + 3 * D_CONV_A, D_CONV_A))).astype(BF16)
        yield

        xbc_ref[lo:lo + L, :] = pcol(OFF_XBC, SSD_CONV_DIM)
        conv_b = vec(ROW_SSD_CONV + SSD_CONV_WIDTH - 1, SSD_CONV_DIM) * xbc_ref[lo:lo + L, :]
        for k in range(1, SSD_CONV_WIDTH):
            conv_b += (vec(ROW_SSD_CONV + SSD_CONV_WIDTH - 1 - k, SSD_CONV_DIM)
                       * xbc_ref[lo - k:lo - k + L, :])
        xbc = _silu(conv_b + vec(ROW_SSD_CONV_B, SSD_CONV_DIM))
        xs = xbc[:, 0:D_SSD]
        bs = xbc[:, D_SSD:D_SSD + SSD_BC]
        cs = xbc[:, D_SSD + SSD_BC:SSD_CONV_DIM]
        yield
        dt_slab = _softplus(jnp.where(head_lane, pcol(OFF_SDT, LANES) + vec(ROW_DT_BIAS, LANES), 0.0))
        cum_slab = _dot_pieces(tril_ref[...], _split3(dt_slab * a_neg2))
        yield
        dt_c = _dot(dt_slab.astype(BF16), expand_ref[...])
        cum = _dot_pieces(_split3(cum_slab)[:2], expand_ref[...])
        tot = cum[L - 1:L, :]
        xd = xs * dt_c
        xd_b = xd.astype(BF16)
        bs_b = bs.astype(BF16)
        cs_b = cs.astype(BF16)
        state_b = state_ref[...].astype(BF16)
        zero_b = jnp.zeros((), BF16)

        bs_blocks = jnp.concatenate([jnp.where(grp0_state, bs_b, zero_b), jnp.where(grp0_state, zero_b, bs_b)], axis=0)
        cb_all = _dot_nt(cs_b, bs_blocks)
        yield
        y_diag_slabs = []
        for s in range(D_SSD // LANES):
            cum_slab = cum[:, s * LANES:(s + 1) * LANES]
            cum_rolled = pltpu.roll(cum_slab, SSD_HEAD_DIM, 1)
            cum_t = cum_slab.T
            pair = []
            for e in range(2):
                g = (2 * s + e) // (SSD_HEADS // SSD_GROUPS)
                col = jnp.where(lane_lo, cum_slab, cum_rolled) if e == 0 else jnp.where(lane_lo, cum_rolled, cum_slab)
                row = cum_t[e * SSD_HEAD_DIM:e * SSD_HEAD_DIM + 1, :]
                decay = jnp.exp2(jnp.where(causal, col - row, -jnp.inf))
                pair.append((cb_all[:, g * L:(g + 1) * L] * decay).astype(BF16))
            xd_slab = xd_b[:, s * LANES:(s + 1) * LANES]
            xd_pair = jnp.concatenate([jnp.where(lane_lo, xd_slab, zero_b), jnp.where(lane_lo, zero_b, xd_slab)], axis=0)
            y_diag_slabs.append(_dot(jnp.concatenate(pair, axis=1), xd_pair))
            yield
        y_diag = jnp.concatenate(y_diag_slabs, axis=1)

        state_blocks = jnp.concatenate([jnp.where(grp0, state_b, zero_b), jnp.where(grp0, zero_b, state_b)], axis=0)
        y_off = _dot(cs_b, state_blocks) * jnp.exp2(cum)

        xdd = (xd * jnp.exp2(tot - cum)).astype(BF16)
        xdd_blocks = jnp.concatenate([jnp.where(grp0, xdd, zero_b), jnp.where(grp0, zero_b, xdd)], axis=0)
        bs_t = jnp.concatenate([bs[:, g * SSD_STATE:(g + 1) * SSD_STATE].T for g in range(SSD_GROUPS)],
                               axis=1).astype(BF16)
        state_ref[...] = state_ref[...] * jnp.exp2(tot) + _dot(bs_t, xdd_blocks)
        yield

        gated = (y_diag + y_off + xs * vec(ROW_D_SKIP, D_SSD)) * _silu(pcol(OFF_SZ, D_SSD))
        g2 = gated * gated
        ss_all = jnp.sum(g2, axis=-1, keepdims=True)
        ss0 = jnp.sum(jnp.where(grp0, g2, 0.0), axis=-1, keepdims=True)
        inv0 = lax.rsqrt(ss0 * (1.0 / half) + SSD_NORM_EPS)
        inv1 = lax.rsqrt((ss_all - ss0) * (1.0 / half) + SSD_NORM_EPS)
        y_b = gated * jnp.where(grp0, inv0, inv1) * vec(ROW_SSD_NORM_G, D_SSD)
        ycat_ref[rows, D_CONV_A:D_CONV_A + D_SSD] = y_b.astype(BF16)
        yield

    scale = (QK_NOPE + QK_ROPE) ** -0.5 * LOG2E
    r0 = ROPE_ROW0
    kb = KEY_BLOCK

    def attend_prepare():
        ang = pos_ref[...].astype(F32) * invf_ref[...]
        cos_t = jnp.cos(ang)
        sin_t = jnp.sin(ang)

        def rope_rows(t1, t2):
            return t1 * cos_t - t2 * sin_t, t2 * cos_t + t1 * sin_t

        c_qa = ptile(OFF_QA, Q_LORA)
        qan = (c_qa * lax.rsqrt(jnp.mean(c_qa * c_qa, axis=-1, keepdims=True) + NORM_EPS)
               * vec(ROW_Q_NORM_G, Q_LORA)).astype(BF16)
        q_t = _dot_nt(wqbt_ref[...], qan)
        yield
        for hd in range(MLA_HEADS):
            base = hd * HEAD_SLAB
            q1, q2 = rope_rows(q_t[base + r0:base + r0 + ROPE_HALF, :],
                               q_t[base + r0 + ROPE_HALF:base + r0 + QK_ROPE, :])
            slab = jnp.concatenate([q_t[base:base + r0, :], q1, q2,
                                    q_t[base + r0 + QK_ROPE:base + HEAD_SLAB, :]], axis=0)
            qt_ref[base:base + HEAD_SLAB, :] = (slab * scale).astype(BF16)
        yield

        c_kv = ptile(OFF_KV, KV_LORA)
        kvn = (c_kv * lax.rsqrt(jnp.mean(c_kv * c_kv, axis=-1, keepdims=True) + NORM_EPS)
               * vec(ROW_KV_NORM_G, KV_LORA)).astype(BF16)
        kr_slab0 = OFF_KR // LANES * LANES
        kr_lane = lax.broadcasted_iota(jnp.int32, (tile, LANES), 1)
        kr_rolled = pltpu.roll(ptile(kr_slab0, LANES), r0 - (OFF_KR - kr_slab0), 1)
        ckr_t = jnp.where((kr_lane >= r0) & (kr_lane < r0 + QK_ROPE), kr_rolled, 0.0).T
        k1, k2 = rope_rows(ckr_t[r0:r0 + ROPE_HALF, :], ckr_t[r0 + ROPE_HALF:r0 + QK_ROPE, :])
        kr_nat = jnp.concatenate([ckr_t[0:r0, :], k1, k2, ckr_t[r0 + QK_ROPE:HEAD_SLAB, :]], axis=0).T
        yield
        k_nat = _dot(kvn, wk_ref[...])
        v_t = _dot_nt(wvt_ref[...], kvn)
        ones_row = lax.broadcasted_iota(jnp.int32, (V_SLAB, tile), 0) == ONES_ROW
        for hd in range(MLA_HEADS):
            base = hd * HEAD_SLAB
            kc_ref[j, :, base:base + HEAD_SLAB] = (k_nat[:, base:base + HEAD_SLAB] + kr_nat).astype(BF16)
            vt_ref[j, hd * V_SLAB:(hd + 1) * V_SLAB, :] = jnp.where(
                ones_row, 1.0, v_t[hd * V_SLAB:(hd + 1) * V_SLAB, :]).astype(BF16)
        m_ref[...] = jnp.full(m_ref.shape, -jnp.inf, F32)
        acc_ref[...] = jnp.zeros(acc_ref.shape, F32)
        yield

    def attend(i, masked, lookahead):
        def spans(sub):
            start = sub * kb if masked else 0
            cuts = [start] + [q for q in range(0, tile + 1, QUERY_BLOCK) if q > start]
            return list(zip(cuts[:-1], cuts[1:]))

        def scores(sub, q0, q1, hd):
            base = hd * HEAD_SLAB
            return _dot(kc_ref[i, sub * kb:(sub + 1) * kb, base:base + HEAD_SLAB],
                        qt_ref[base:base + HEAD_SLAB, q0:q1])

        steps = [(sub, q0, q1, hd) for sub in range(tile // kb) for q0, q1 in spans(sub) for hd in range(MLA_HEADS)]
        pending = [scores(*st) for st in steps[:lookahead]]
        for n, (sub, q0, q1, hd) in enumerate(steps):
            s_t = pending.pop(0)
            if n + lookahead < len(steps):
                pending.append(scores(*steps[n + lookahead]))
            vrows = slice(hd * V_SLAB, (hd + 1) * V_SLAB)
            if masked and q0 < (sub + 1) * kb:
                visible = (lax.broadcasted_iota(jnp.int32, s_t.shape, 0) + (sub * kb - q0)
                           <= lax.broadcasted_iota(jnp.int32, s_t.shape, 1))
                s_t = jnp.where(visible, s_t, -jnp.inf)
            m_old = m_ref[hd, :, q0:q1]
            m_new = jnp.maximum(m_old, jnp.max(s_t, axis=0, keepdims=True))
            p = jnp.exp2(s_t - m_new).astype(BF16)
            alpha = jnp.exp2(m_old - m_new)
            acc_ref[vrows, q0:q1] = (alpha * acc_ref[vrows, q0:q1]
                                     + _dot(vt_ref[i, vrows, sub * kb:(sub + 1) * kb], p))
            m_ref[hd, :, q0:q1] = m_new
            yield

    def attend_diagonal():
        yield from attend_prepare()
        yield from attend(j, True, DIAG_LOOKAHEAD)

    d_ab = D_CONV_A + D_SSD

    def out_partial(chunks):
        for c in chunks:
            rows = slice(c * L, (c + 1) * L)
            o_ref[rows, :] = x_ref[rows, :] + _dot(ycat_ref[rows, 0:d_ab], w_out_ref[0:d_ab, :])
            yield

    _interleave(project(0))
    for c in range(n_chunks - 1):
        _interleave(project(c + 1), mix(c))
    _interleave((attend_diagonal(), DIAG_STAGES_PER_MIX_STAGE), mix(n_chunks - 1),
                out_partial(range(n_chunks - 1)))
    _interleave(out_partial([n_chunks - 1]))
    ua_ref[0:HALO, :] = ua_ref[tile:tile + HALO, :]
    xbc_ref[0:HALO, :] = xbc_ref[tile:tile + HALO, :]

    def body(i, carry):
        _interleave(attend(i, False, LOOP_LOOKAHEAD))
        return carry

    lax.fori_loop(0, j, body, 0)

    for hd in range(MLA_HEADS):
        base = hd * V_SLAB
        yt_ref[hd * V_DIM:(hd + 1) * V_DIM, :] = (acc_ref[base:base + V_DIM, :]
                                                  / acc_ref[base + ONES_ROW:base + ONES_ROW + 1, :])
    c_z = ptile(OFF_CZ, D_MLA)
    ycat_ref[:, D_CONV_A + D_SSD:D_MIX] = (yt_ref[...].T * _silu(c_z)).astype(BF16)

    out = o_ref[...] + _dot(ycat_ref[:, d_ab:D_MIX], w_out_ref[d_ab:D_MIX, :])
    if final:
        out = (out * lax.rsqrt(jnp.mean(out * out, axis=-1, keepdims=True) + NORM_EPS)
               * vec(ROW_FINAL_G, D_MODEL))
    o_ref[...] = out


def _prep_layer(norm_g, conv_a_w, ssd_conv_w, ssd_conv_b, ssd_dt_bias, ssd_a_log, ssd_d,
                ssd_norm_g, mla_q_norm_g, w_qb, mla_kv_norm_g, w_kvb, final_norm_g):
    wqb = w_qb.reshape(Q_LORA, MLA_HEADS, QK_NOPE + QK_ROPE)
    wqbt = jnp.pad(wqb, ((0, 0), (0, 0), (0, HEAD_SLAB - QK_NOPE - QK_ROPE))).reshape(
        Q_LORA, MLA_HEADS * HEAD_SLAB).T.astype(BF16)
    wkv = w_kvb.reshape(KV_LORA, MLA_HEADS, QK_NOPE + V_DIM)
    wk = jnp.pad(wkv[:, :, :QK_NOPE], ((0, 0), (0, 0), (0, HEAD_SLAB - QK_NOPE))).reshape(
        KV_LORA, MLA_HEADS * HEAD_SLAB).astype(BF16)
    wvt = jnp.pad(wkv[:, :, QK_NOPE:], ((0, 0), (0, 0), (0, V_SLAB - V_DIM))).reshape(
        KV_LORA, MLA_HEADS * V_SLAB).T.astype(BF16)

    def row(v):
        return jnp.pad(v.astype(F32), (0, D_MODEL - v.shape[0]))[None, :]

    rows = [row(norm_g)]
    rows += [row(conv_a_w[k]) for k in range(CONV_A_WIDTH)]
    rows += [row(ssd_conv_w[k]) for k in range(SSD_CONV_WIDTH)]
    rows += [row(ssd_conv_b), row(ssd_dt_bias), row(ssd_a_log), row(jnp.repeat(ssd_d, SSD_HEAD_DIM)),
             row(ssd_norm_g), row(mla_q_norm_g), row(mla_kv_norm_g), row(final_norm_g)]
    rows += [jnp.zeros((N_VEC_ROWS - len(rows), D_MODEL), F32)]
    vecs = jnp.concatenate(rows, axis=0)
    return vecs, wqbt, wk, wvt


def _vmem_limit(tile, n_tiles):
    f32, bf16 = 4, 2
    blocks = 2 * (2 * tile * D_MODEL * f32)
    weights = 2 * bf16 * (D_MODEL * N_PROJ + Q_LORA * MLA_HEADS * HEAD_SLAB
                          + KV_LORA * MLA_HEADS * (HEAD_SLAB + V_SLAB) + D_MIX * D_MODEL)
    scratch = (tile * N_PROJ * f32 + (tile + HALO) * (D_CONV_A + SSD_CONV_DIM) * f32
               + SSD_STATE * D_SSD * f32 + n_tiles * tile * MLA_HEADS * (HEAD_SLAB + V_SLAB) * bf16
               + MLA_HEADS * HEAD_SLAB * tile * bf16 + MLA_HEADS * V_SLAB * tile * f32
               + D_MLA * tile * f32 + tile * D_MIX * bf16)
    temporaries = 2 * tile * N_PROJ * f32
    return min(blocks + weights + scratch + temporaries, V7X_VMEM_BYTES - 8 * 1024 * 1024)


def _layer_call(x, pos3, invf, params, w_in_all, w_out_all, *, layer, final):
    batch, seq, d_model = x.shape
    tile = SEQ_TILE
    n_tiles = seq // tile
    vecs, wqbt, wk, wvt = params
    const = lambda shape: pl.BlockSpec(shape, lambda b, j: (0,) * len(shape))
    of_layer = lambda a: pl.BlockSpec((None,) + a.shape[1:], lambda b, j: (layer, 0, 0))
    return pl.pallas_call(
        functools.partial(_layer_kernel, final=final),
        out_shape=jax.ShapeDtypeStruct(x.shape, F32),
        grid=(batch, n_tiles),
        in_specs=[
            pl.BlockSpec((None, 1, tile), lambda b, j: (b, 0, j)),
            pl.BlockSpec((None, tile, d_model), lambda b, j: (b, j, 0)),
            const(invf.shape), const(vecs.shape), of_layer(w_in_all),
            const(wqbt.shape), const(wk.shape), const(wvt.shape), of_layer(w_out_all),
        ],
        out_specs=pl.BlockSpec((None, tile, d_model), lambda b, j: (b, j, 0)),
        scratch_shapes=[
            pltpu.VMEM((tile + HALO, D_CONV_A), F32),
            pltpu.VMEM((tile + HALO, SSD_CONV_DIM), F32),
            pltpu.VMEM((SSD_STATE, D_SSD), F32),
            pltpu.VMEM((n_tiles, tile, MLA_HEADS * HEAD_SLAB), BF16),
            pltpu.VMEM((n_tiles, MLA_HEADS * V_SLAB, tile), BF16),
            pltpu.VMEM((MLA_HEADS * HEAD_SLAB, tile), BF16),
            pltpu.VMEM((MLA_HEADS * V_SLAB, tile), F32),
            pltpu.VMEM((MLA_HEADS, 1, tile), F32),
            pltpu.VMEM((D_MLA, tile), F32),
            pltpu.VMEM((tile, D_MIX), BF16),
            pltpu.VMEM((SSD_CHUNK, SSD_CHUNK), BF16),
            pltpu.VMEM((LANES, D_SSD), BF16),
        ] + [pltpu.VMEM((SSD_CHUNK, N_PROJ), F32)] * (tile // SSD_CHUNK),
        compiler_params=pltpu.CompilerParams(
            dimension_semantics=("arbitrary", "arbitrary"),
            vmem_limit_bytes=_vmem_limit(tile, n_tiles)),
        name="hybrid_layer_final" if final else "hybrid_layer",
    )(pos3, x, invf, vecs, w_in_all, wqbt, wk, wvt, w_out_all)


def kernel(x, positions, norm_g, w_in, conv_a_w, ssd_conv_w, ssd_conv_b, ssd_dt_bias, ssd_a_log, ssd_d,
           ssd_norm_g, mla_q_norm_g, w_qb, mla_kv_norm_g, w_kvb, w_out, final_norm_g):
    batch, seq, _ = x.shape
    depth = norm_g.shape[0]
    assert seq % SEQ_TILE == 0 and SEQ_TILE % SSD_CHUNK == 0 and SEQ_TILE % KEY_BLOCK == 0
    inv_freq = ROPE_BASE ** (-jnp.arange(0, QK_ROPE, 2, dtype=F32) / QK_ROPE)
    invf = jnp.broadcast_to(inv_freq[:, None], (ROPE_HALF, SEQ_TILE))
    pos3 = positions.reshape(batch, 1, seq)
    w_in_all = jnp.pad(w_in.astype(BF16), ((0, 0), (0, 0), (0, N_PROJ - IN_COLS)))
    w_out_all = w_out.astype(BF16)
    for l in range(depth):
        params = _prep_layer(norm_g[l], conv_a_w[l], ssd_conv_w[l], ssd_conv_b[l], ssd_dt_bias[l],
                             ssd_a_log[l], ssd_d[l], ssd_norm_g[l], mla_q_norm_g[l], w_qb[l],
                             mla_kv_norm_g[l], w_kvb[l], final_norm_g)
        x = _layer_call(x, pos3, invf, params, w_in_all, w_out_all, layer=l, final=(l == depth - 1))
    return x
```

```python
import functools

import jax
import jax.numpy as jnp
from jax import lax
from jax.experimental import pallas as pl
from jax.experimental.pallas import tpu as pltpu

F32 = jnp.float32
BF16 = jnp.bfloat16

D_MODEL = 1024
D_CONV_A = 256
CONV_A_WIDTH = 3
SSD_HEADS = 6
SSD_HEAD_DIM = 64
D_SSD = SSD_HEADS * SSD_HEAD_DIM
SSD_GROUPS = 2
SSD_STATE = 128
SSD_CONV_WIDTH = 4
SSD_BC = SSD_GROUPS * SSD_STATE
SSD_CONV_DIM = D_SSD + 2 * SSD_BC
SSD_NORM_EPS = 1e-5
MLA_HEADS = 6
Q_LORA = 256
KV_LORA = 128
QK_NOPE = 64
QK_ROPE = 32
V_DIM = 64
D_MLA = MLA_HEADS * V_DIM
ROPE_BASE = 10000.0
D_MIX = D_CONV_A + D_SSD + D_MLA
NORM_EPS = 1e-6

LANES = 128
SUBLANES = 8
V7X_VMEM_BYTES = 64 * 1024 * 1024

SEQ_TILE = 512
SSD_CHUNK = 128
HEAD_SLAB = LANES
ROPE_HALF = QK_ROPE // 2
ROPE_ROW0 = QK_NOPE
ONES_ROW = V_DIM
V_SLAB = 80
KEY_BLOCK = 256
QUERY_BLOCK = 256
DIAG_LOOKAHEAD = 4
LOOP_LOOKAHEAD = 6
PROJ_GROUP = 512
DIAG_STAGES_PER_MIX_STAGE = 3
LOG2E = 1.4426950408889634
HALO = SUBLANES

OFF_A = 0
OFF_SZ = OFF_A + 4 * D_CONV_A
OFF_XBC = OFF_SZ + D_SSD
OFF_SDT = OFF_XBC + SSD_CONV_DIM
OFF_QA = OFF_SDT + SSD_HEADS
OFF_KV = OFF_QA + Q_LORA
OFF_KR = OFF_KV + KV_LORA
OFF_CZ = OFF_KR + QK_ROPE
IN_COLS = OFF_CZ + D_MLA
N_PROJ = -(-IN_COLS // LANES) * LANES

ROW_NORM_G = 0
ROW_CONV_A = 1
ROW_SSD_CONV = ROW_CONV_A + CONV_A_WIDTH
ROW_SSD_CONV_B = ROW_SSD_CONV + SSD_CONV_WIDTH
ROW_DT_BIAS = ROW_SSD_CONV_B + 1
ROW_A_LOG = ROW_DT_BIAS + 1
ROW_D_SKIP = ROW_A_LOG + 1
ROW_SSD_NORM_G = ROW_D_SKIP + 1
ROW_Q_NORM_G = ROW_SSD_NORM_G + 1
ROW_KV_NORM_G = ROW_Q_NORM_G + 1
ROW_FINAL_G = ROW_KV_NORM_G + 1
N_VEC_ROWS = 16


def _silu(v):
    return v * (1.0 / (1.0 + jnp.exp(-v)))


def _softplus(v):
    return jnp.maximum(v, 0.0) + jnp.log1p(jnp.exp(-jnp.abs(v)))


def _split3(v):
    hi = v.astype(BF16)
    r1 = v - hi.astype(F32)
    mid = r1.astype(BF16)
    lo = (r1 - mid.astype(F32)).astype(BF16)
    return hi, mid, lo


def _dot(a, b):
    return jnp.dot(a, b, preferred_element_type=F32)


def _dot_pieces(a, b):
    if isinstance(a, tuple):
        return functools.reduce(lambda acc, piece: acc + _dot(piece, b), a[1:], _dot(a[0], b))
    return functools.reduce(lambda acc, piece: acc + _dot(a, piece), b[1:], _dot(a, b[0]))


def _dot_nt(a, b):
    return lax.dot_general(a, b, (((1,), (1,)), ((), ())), preferred_element_type=F32)


def _interleave(*stage_iters):
    live = [it if isinstance(it, tuple) else (it, 1) for it in stage_iters]
    while live:
        for entry in list(live):
            it, per_round = entry
            for _ in range(per_round):
                if next(it, StopIteration) is StopIteration:
                    live.remove(entry)
                    break


def _layer_kernel(pos_ref, x_ref, invf_ref, vec_ref, w_in_ref, wqbt_ref, wk_ref,
                  wvt_ref, w_out_ref, o_ref,
                  ua_ref, xbc_ref, state_ref, kc_ref, vt_ref, qt_ref, acc_ref,
                  m_ref, yt_ref, ycat_ref, tril_ref, expand_ref, *proj_refs, final):
    j = pl.program_id(1)
    tile = x_ref.shape[0]

    def ptile(off, width):
        return jnp.concatenate([p[:, off:off + width] for p in proj_refs], axis=0)

    def vec(row, width):
        return vec_ref[row:row + 1, 0:width]

    L = SSD_CHUNK
    n_chunks = tile // L

    @pl.when(j == 0)
    def _():
        ua_ref[0:HALO, :] = jnp.zeros((HALO, D_CONV_A), F32)
        xbc_ref[0:HALO, :] = jnp.zeros((HALO, SSD_CONV_DIM), F32)
        state_ref[...] = jnp.zeros(state_ref.shape, F32)

    row_i = lax.broadcasted_iota(jnp.int32, (L, L), 0)
    col_i = lax.broadcasted_iota(jnp.int32, (L, L), 1)
    causal = row_i >= col_i
    tril_ref[...] = jnp.where(causal, 1.0, 0.0).astype(BF16)
    lane_lo = lax.broadcasted_iota(jnp.int32, (L, LANES), 1) < SSD_HEAD_DIM
    half = D_SSD // SSD_GROUPS
    grp0 = lax.broadcasted_iota(jnp.int32, (1, D_SSD), 1) < half
    grp0_state = lax.broadcasted_iota(jnp.int32, (1, SSD_BC), 1) < SSD_STATE
    head_lane = lax.broadcasted_iota(jnp.int32, (1, LANES), 1) < SSD_HEADS
    a_neg2 = -jnp.exp(vec(ROW_A_LOG, LANES)) * LOG2E
    head_of_lane = lax.shift_right_logical(lax.broadcasted_iota(jnp.int32, (LANES, D_SSD), 1),
                                           jnp.int32(SSD_HEAD_DIM.bit_length() - 1))
    expand_ref[...] = jnp.where(lax.broadcasted_iota(jnp.int32, (LANES, D_SSD), 0) == head_of_lane,
                                1.0, 0.0).astype(BF16)

    def project(c):
        rows = slice(c * L, (c + 1) * L)
        x = x_ref[rows, :]
        h = x * lax.rsqrt(jnp.mean(x * x, axis=-1, keepdims=True) + NORM_EPS) * vec(ROW_NORM_G, D_MODEL)
        hb = h.astype(BF16)
        yield
        for g0 in range(0, N_PROJ, PROJ_GROUP):
            g1 = min(g0 + PROJ_GROUP, N_PROJ)
            proj_refs[c][:, g0:g1] = _dot(hb, w_in_ref[:, g0:g1])
            yield

    def mix(c):
        rows = slice(c * L, (c + 1) * L)
        lo = HALO + c * L

        def pcol(off, width):
            return proj_refs[c][:, off:off + width]

        ua_ref[lo:lo + L, :] = pcol(OFF_A + 2 * D_CONV_A, D_CONV_A) * pcol(OFF_A, D_CONV_A)
        conv_a = vec(ROW_CONV_A + CONV_A_WIDTH - 1, D_CONV_A) * ua_ref[lo:lo + L, :]
        for k in range(1, CONV_A_WIDTH):
            conv_a += vec(ROW_CONV_A + CONV_A_WIDTH - 1 - k, D_CONV_A) * ua_ref[lo - k:lo - k + L, :]
        ycat_ref[rows, 0:D_CONV_A] = (pcol(OFF_A + D_CONV_A, D_CONV_A) * conv_a
                                      * _silu(pcol(OFF_A + 3 * D_CONV_A, D_CONV_A))).astype(BF16)
        yield

        xbc_ref[lo:lo + L, :] = pcol(OFF_XBC, SSD_CONV_DIM)
        conv_b = vec(ROW_SSD_CONV + SSD_CONV_WIDTH - 1, SSD_CONV_DIM) * xbc_ref[lo:lo + L, :]
        for k in range(1, SSD_CONV_WIDTH):
            conv_b += (vec(ROW_SSD_CONV + SSD_CONV_WIDTH - 1 - k, SSD_CONV_DIM)
                       * xbc_ref[lo - k:lo - k + L, :])
        xbc = _silu(conv_b + vec(ROW_SSD_CONV_B, SSD_CONV_DIM))
        xs = xbc[:, 0:D_SSD]
        bs = xbc[:, D_SSD:D_SSD + SSD_BC]
        cs = xbc[:, D_SSD + SSD_BC:SSD_CONV_DIM]
        yield
        dt_slab = _softplus(jnp.where(head_lane, pcol(OFF_SDT, LANES) + vec(ROW_DT_BIAS, LANES), 0.0))
        cum_slab = _dot_pieces(tril_ref[...], _split3(dt_slab * a_neg2))
        yield
        dt_c = _dot(dt_slab.astype(BF16), expand_ref[...])
        cum = _dot_pieces(_split3(cum_slab)[:2], expand_ref[...])
        tot = cum[L - 1:L, :]
        xd = xs * dt_c
        xd_b = xd.astype(BF16)
        bs_b = bs.astype(BF16)
        cs_b = cs.astype(BF16)
        state_b = state_ref[...].astype(BF16)
        zero_b = jnp.zeros((), BF16)

        bs_blocks = jnp.concatenate([jnp.where(grp0_state, bs_b, zero_b), jnp.where(grp0_state, zero_b, bs_b)], axis=0)
        cb_all = _dot_nt(cs_b, bs_blocks)
        yield
        y_diag_slabs = []
        for s in range(D_SSD // LANES):
            cum_slab = cum[:, s * LANES:(s + 1) * LANES]
            cum_rolled = pltpu.roll(cum_slab, SSD_HEAD_DIM, 1)
            cum_t = cum_slab.T
            pair = []
            for e in range(2):
                g = (2 * s + e) // (SSD_HEADS // SSD_GROUPS)
                col = jnp.where(lane_lo, cum_slab, cum_rolled) if e == 0 else jnp.where(lane_lo, cum_rolled, cum_slab)
                row = cum_t[e * SSD_HEAD_DIM:e * SSD_HEAD_DIM + 1, :]
                decay = jnp.exp2(jnp.where(causal, col - row, -jnp.inf))
                pair.append((cb_all[:, g * L:(g + 1) * L] * decay).astype(BF16))
            xd_slab = xd_b[:, s * LANES:(s + 1) * LANES]
            xd_pair = jnp.concatenate([jnp.where(lane_lo, xd_slab, zero_b), jnp.where(lane_lo, zero_b, xd_slab)], axis=0)
            y_diag_slabs.append(_dot(jnp.concatenate(pair, axis=1), xd_pair))
            yield
        y_diag = jnp.concatenate(y_diag_slabs, axis=1)

        state_blocks = jnp.concatenate([jnp.where(grp0, state_b, zero_b), jnp.where(grp0, zero_b, state_b)], axis=0)
        y_off = _dot(cs_b, state_blocks) * jnp.exp2(cum)

        xdd = (xd * jnp.exp2(tot - cum)).astype(BF16)
        xdd_blocks = jnp.concatenate([jnp.where(grp0, xdd, zero_b), jnp.where(grp0, zero_b, xdd)], axis=0)
        bs_t = jnp.concatenate([bs[:, g * SSD_STATE:(g + 1) * SSD_STATE].T for g in range(SSD_GROUPS)],
                               axis=1).astype(BF16)
        state_ref[...] = state_ref[...] * jnp.exp2(tot) + _dot(bs_t, xdd_blocks)
        yield

        gated = (y_diag + y_off + xs * vec(ROW_D_SKIP, D_SSD)) * _silu(pcol(OFF_SZ, D_SSD))
        g2 = gated * gated
        ss_all = jnp.sum(g2, axis=-1, keepdims=True)
        ss0 = jnp.sum(jnp.where(grp0, g2, 0.0), axis=-1, keepdims=True)
        inv0 = lax.rsqrt(ss0 * (1.0 / half) + SSD_NORM_EPS)
        inv1 = lax.rsqrt((ss_all - ss0) * (1.0 / half) + SSD_NORM_EPS)
        y_b = gated * jnp.where(grp0, inv0, inv1) * vec(ROW_SSD_NORM_G, D_SSD)
        ycat_ref[rows, D_CONV_A:D_CONV_A + D_SSD] = y_b.astype(BF16)
        yield

    scale = (QK_NOPE + QK_ROPE) ** -0.5 * LOG2E
    r0 = ROPE_ROW0
    kb = KEY_BLOCK

    def attend_prepare():
        ang = pos_ref[...].astype(F32) * invf_ref[...]
        cos_t = jnp.cos(ang)
        sin_t = jnp.sin(ang)

        def rope_rows(t1, t2):
            return t1 * cos_t - t2 * sin_t, t2 * cos_t + t1 * sin_t

        c_qa = ptile(OFF_QA, Q_LORA)
        qan = (c_qa * lax.rsqrt(jnp.mean(c_qa * c_qa, axis=-1, keepdims=True) + NORM_EPS)
               * vec(ROW_Q_NORM_G, Q_LORA)).astype(BF16)
        q_t = _dot_nt(wqbt_ref[...], qan)
        yield
        for hd in range(MLA_HEADS):
            base = hd * HEAD_SLAB
            q1, q2 = rope_rows(q_t[base + r0:base + r0 + ROPE_HALF, :],
                               q_t[base + r0 + ROPE_HALF:base + r0 + QK_ROPE, :])
            slab = jnp.concatenate([q_t[base:base + r0, :], q1, q2,
                                    q_t[base + r0 + QK_ROPE:base + HEAD_SLAB, :]], axis=0)
            qt_ref[base:base + HEAD_SLAB, :] = (slab * scale).astype(BF16)
        yield

        c_kv = ptile(OFF_KV, KV_LORA)
        kvn = (c_kv * lax.rsqrt(jnp.mean(c_kv * c_kv, axis=-1, keepdims=True) + NORM_EPS)
               * vec(ROW_KV_NORM_G, KV_LORA)).astype(BF16)
        kr_slab0 = OFF_KR // LANES * LANES
        kr_lane = lax.broadcasted_iota(jnp.int32, (tile, LANES), 1)
        kr_rolled = pltpu.roll(ptile(kr_slab0, LANES), r0 - (OFF_KR - kr_slab0), 1)
        ckr_t = jnp.where((kr_lane >= r0) & (kr_lane < r0 + QK_ROPE), kr_rolled, 0.0).T
        k1, k2 = rope_rows(ckr_t[r0:r0 + ROPE_HALF, :], ckr_t[r0 + ROPE_HALF:r0 + QK_ROPE, :])
        kr_nat = jnp.concatenate([ckr_t[0:r0, :], k1, k2, ckr_t[r0 + QK_ROPE:HEAD_SLAB, :]], axis=0).T
        yield
        k_nat = _dot(kvn, wk_ref[...])
        v_t = _dot_nt(wvt_ref[...], kvn)
        ones_row = lax.broadcasted_iota(jnp.int32, (V_SLAB, tile), 0) == ONES_ROW
        for hd in range(MLA_HEADS):
            base = hd * HEAD_SLAB
            kc_ref[j, :, base:base + HEAD_SLAB] = (k_nat[:, base:base + HEAD_SLAB] + kr_nat).astype(BF16)
            vt_ref[j, hd * V_SLAB:(hd + 1) * V_SLAB, :] = jnp.where(
                ones_row, 1.0, v_t[hd * V_SLAB:(hd + 1) * V_SLAB, :]).astype(BF16)
        m_ref[...] = jnp.full(m_ref.shape, -jnp.inf, F32)
        acc_ref[...] = jnp.zeros(acc_ref.shape, F32)
        yield

    def attend(i, masked, lookahead):
        def spans(sub):
            start = sub * kb if masked else 0
            cuts = [start] + [q for q in range(0, tile + 1, QUERY_BLOCK) if q > start]
            return list(zip(cuts[:-1], cuts[1:]))

        def scores(sub, q0, q1, hd):
            base = hd * HEAD_SLAB
            return _dot(kc_ref[i, sub * kb:(sub + 1) * kb, base:base + HEAD_SLAB],
                        qt_ref[base:base + HEAD_SLAB, q0:q1])

        steps = [(sub, q0, q1, hd) for sub in range(tile // kb) for q0, q1 in spans(sub) for hd in range(MLA_HEADS)]
        pending = [scores(*st) for st in steps[:lookahead]]
        for n, (sub, q0, q1, hd) in enumerate(steps):
            s_t = pending.pop(0)
            if n + lookahead < len(steps):
                pending.append(scores(*steps[n + lookahead]))
            vrows = slice(hd * V_SLAB, (hd + 1) * V_SLAB)
            if masked and q0 < (sub + 1) * kb:
                visible = (lax.broadcasted_iota(jnp.int32, s_t.shape, 0) + (sub * kb - q0)
                           <= lax.broadcasted_iota(jnp.int32, s_t.shape, 1))
                s_t = jnp.where(visible, s_t, -jnp.inf)
            m_old = m_ref[hd, :, q0:q1]
            m_new = jnp.maximum(m_old, jnp.max(s_t, axis=0, keepdims=True))
            p = jnp.exp2(s_t - m_new).astype(BF16)
            alpha = jnp.exp2(m_old - m_new)
            acc_ref[vrows, q0:q1] = (alpha * acc_ref[vrows, q0:q1]
                                     + _dot(vt_ref[i, vrows, sub * kb:(sub + 1) * kb], p))
            m_ref[hd, :, q0:q1] = m_new
            yield

    def attend_diagonal():
        yield from attend_prepare()
        yield from attend(j, True, DIAG_LOOKAHEAD)

    d_ab = D_CONV_A + D_SSD

    def out_partial(chunks):
        for c in chunks:
            rows = slice(c * L, (c + 1) * L)
            o_ref[rows, :] = x_ref[rows, :] + _dot(ycat_ref[rows, 0:d_ab], w_out_ref[0:d_ab, :])
            yield

    _interleave(project(0))
    for c in range(n_chunks - 1):
        _interleave(project(c + 1), mix(c))
    _interleave((attend_diagonal(), DIAG_STAGES_PER_MIX_STAGE), mix(n_chunks - 1),
                out_partial(range(n_chunks - 1)))
    _interleave(out_partial([n_chunks - 1]))
    ua_ref[0:HALO, :] = ua_ref[tile:tile + HALO, :]
    xbc_ref[0:HALO, :] = xbc_ref[tile:tile + HALO, :]

    def body(i, carry):
        _interleave(attend(i, False, LOOP_LOOKAHEAD))
        return carry

    lax.fori_loop(0, j, body, 0)

    for hd in range(MLA_HEADS):
        base = hd * V_SLAB
        yt_ref[hd * V_DIM:(hd + 1) * V_DIM, :] = (acc_ref[base:base + V_DIM, :]
                                                  / acc_ref[base + ONES_ROW:base + ONES_ROW + 1, :])
    c_z = ptile(OFF_CZ, D_MLA)
    ycat_ref[:, D_CONV_A + D_SSD:D_MIX] = (yt_ref[...].T * _silu(c_z)).astype(BF16)

    out = o_ref[...] + _dot(ycat_ref[:, d_ab:D_MIX], w_out_ref[d_ab:D_MIX, :])
    if final:
        out = (out * lax.rsqrt(jnp.mean(out * out, axis=-1, keepdims=True) + NORM_EPS)
               * vec(ROW_FINAL_G, D_MODEL))
    o_ref[...] = out


def _prep_layer(norm_g, conv_a_w, ssd_conv_w, ssd_conv_b, ssd_dt_bias, ssd_a_log, ssd_d,
                ssd_norm_g, mla_q_norm_g, w_qb, mla_kv_norm_g, w_kvb, final_norm_g):
    wqb = w_qb.reshape(Q_LORA, MLA_HEADS, QK_NOPE + QK_ROPE)
    wqbt = jnp.pad(wqb, ((0, 0), (0, 0), (0, HEAD_SLAB - QK_NOPE - QK_ROPE))).reshape(
        Q_LORA, MLA_HEADS * HEAD_SLAB).T.astype(BF16)
    wkv = w_kvb.reshape(KV_LORA, MLA_HEADS, QK_NOPE + V_DIM)
    wk = jnp.pad(wkv[:, :, :QK_NOPE], ((0, 0), (0, 0), (0, HEAD_SLAB - QK_NOPE))).reshape(
        KV_LORA, MLA_HEADS * HEAD_SLAB).astype(BF16)
    wvt = jnp.pad(wkv[:, :, QK_NOPE:], ((0, 0), (0, 0), (0, V_SLAB - V_DIM))).reshape(
        KV_LORA, MLA_HEADS * V_SLAB).T.astype(BF16)

    def row(v):
        return jnp.pad(v.astype(F32), (0, D_MODEL - v.shape[0]))[None, :]

    rows = [row(norm_g)]
    rows += [row(conv_a_w[k]) for k in range(CONV_A_WIDTH)]
    rows += [row(ssd_conv_w[k]) for k in range(SSD_CONV_WIDTH)]
    rows += [row(ssd_conv_b), row(ssd_dt_bias), row(ssd_a_log), row(jnp.repeat(ssd_d, SSD_HEAD_DIM)),
             row(ssd_norm_g), row(mla_q_norm_g), row(mla_kv_norm_g), row(final_norm_g)]
    rows += [jnp.zeros((N_VEC_ROWS - len(rows), D_MODEL), F32)]
    vecs = jnp.concatenate(rows, axis=0)
    return vecs, wqbt, wk, wvt


def _vmem_limit(tile, n_tiles):
    f32, bf16 = 4, 2
    blocks = 2 * (2 * tile * D_MODEL * f32)
    weights = 2 * bf16 * (D_MODEL * N_PROJ + Q_LORA * MLA_HEADS * HEAD_SLAB
                          + KV_LORA * MLA_HEADS * (HEAD_SLAB + V_SLAB) + D_MIX * D_MODEL)
    scratch = (tile * N_PROJ * f32 + (tile + HALO) * (D_CONV_A + SSD_CONV_DIM) * f32
               + SSD_STATE * D_SSD * f32 + n_tiles * tile * MLA_HEADS * (HEAD_SLAB + V_SLAB) * bf16
               + MLA_HEADS * HEAD_SLAB * tile * bf16 + MLA_HEADS * V_SLAB * tile * f32
               + D_MLA * tile * f32 + tile * D_MIX * bf16)
    temporaries = 2 * tile * N_PROJ * f32
    return min(blocks + weights + scratch + temporaries, V7X_VMEM_BYTES - 8 * 1024 * 1024)


def _layer_call(x, pos3, invf, params, w_in_all, w_out_all, *, layer, final):
    batch, seq, d_model = x.shape
    tile = SEQ_TILE
    n_tiles = seq // tile
    vecs, wqbt, wk, wvt = params
    const = lambda shape: pl.BlockSpec(shape, lambda b, j: (0,) * len(shape))
    of_layer = lambda a: pl.BlockSpec((None,) + a.shape[1:], lambda b, j: (layer, 0, 0))
    return pl.pallas_call(
        functools.partial(_layer_kernel, final=final),
        out_shape=jax.ShapeDtypeStruct(x.shape, F32),
        grid=(batch, n_tiles),
        in_specs=[
            pl.BlockSpec((None, 1, tile), lambda b, j: (b, 0, j)),
            pl.BlockSpec((None, tile, d_model), lambda b, j: (b, j, 0)),
            const(invf.shape), const(vecs.shape), of_layer(w_in_all),
            const(wqbt.shape), const(wk.shape), const(wvt.shape), of_layer(w_out_all),
        ],
        out_specs=pl.BlockSpec((None, tile, d_model), lambda b, j: (b, j, 0)),
        scratch_shapes=[
            pltpu.VMEM((tile + HALO, D_CONV_A), F32),
            pltpu.VMEM((tile + HALO, SSD_CONV_DIM), F32),
            pltpu.VMEM((SSD_STATE, D_SSD), F32),
            pltpu.VMEM((n_tiles, tile, MLA_HEADS * HEAD_SLAB), BF16),
            pltpu.VMEM((n_tiles, MLA_HEADS * V_SLAB, tile), BF16),
            pltpu.VMEM((MLA_HEADS * HEAD_SLAB, tile), BF16),
            pltpu.VMEM((MLA_HEADS * V_SLAB, tile), F32),
            pltpu.VMEM((MLA_HEADS, 1, tile), F32),
            pltpu.VMEM((D_MLA, tile), F32),
            pltpu.VMEM((tile, D_MIX), BF16),
            pltpu.VMEM((SSD_CHUNK, SSD_CHUNK), BF16),
            pltpu.VMEM((LANES, D_SSD), BF16),
        ] + [pltpu.VMEM((SSD_CHUNK, N_PROJ), F32)] * (tile // SSD_CHUNK),
        compiler_params=pltpu.CompilerParams(
            dimension_semantics=("arbitrary", "arbitrary"),
            vmem_limit_bytes=_vmem_limit(tile, n_tiles)),
        name="hybrid_layer_final" if final else "hybrid_layer",
    )(pos3, x, invf, vecs, w_in_all, wqbt, wk, wvt, w_out_all)


def kernel(x, positions, norm_g, w_in, conv_a_w, ssd_conv_w, ssd_conv_b, ssd_dt_bias, ssd_a_log, ssd_d,
           ssd_norm_g, mla_q_norm_g, w_qb, mla_kv_norm_g, w_kvb, w_out, final_norm_g):
    batch, seq, _ = x.shape
    depth = norm_g.shape[0]
    assert seq % SEQ_TILE == 0 and SEQ_TILE % SSD_CHUNK == 0 and SEQ_TILE % KEY_BLOCK == 0
    inv_freq = ROPE_BASE ** (-jnp.arange(0, QK_ROPE, 2, dtype=F32) / QK_ROPE)
    invf = jnp.broadcast_to(inv_freq[:, None], (ROPE_HALF, SEQ_TILE))
    pos3 = positions.reshape(batch, 1, seq)
    w_in_all = jnp.pad(w_in.astype(BF16), ((0, 0), (0, 0), (0, N_PROJ - IN_COLS)))
    w_out_all = w_out.astype(BF16)
    for l in range(depth):
        params = _prep_layer(norm_g[l], conv_a_w[l], ssd_conv_w[l], ssd_conv_b[l], ssd_dt_bias[l],
                             ssd_a_log[l], ssd_d[l], ssd_norm_g[l], mla_q_norm_g[l], w_qb[l],
                             mla_kv_norm_g[l], w_kvb[l], final_norm_g)
        x = _layer_call(x, pos3, invf, params, w_in_all, w_out_all, layer=l, final=(l == depth - 1))
    return x
```

```python
import functools

import jax
import jax.numpy as jnp
from jax import lax
from jax.experimental import pallas as pl
from jax.experimental.pallas import tpu as pltpu

F32 = jnp.float32
BF16 = jnp.bfloat16

D_MODEL = 1024
D_CONV_A = 256
CONV_A_WIDTH = 3
SSD_HEADS = 6
SSD_HEAD_DIM = 64
D_SSD = SSD_HEADS * SSD_HEAD_DIM
SSD_GROUPS = 2
SSD_STATE = 128
SSD_CONV_WIDTH = 4
SSD_BC = SSD_GROUPS * SSD_STATE
SSD_CONV_DIM = D_SSD + 2 * SSD_BC
SSD_NORM_EPS = 1e-5
MLA_HEADS = 6
Q_LORA = 256
KV_LORA = 128
QK_NOPE = 64
QK_ROPE = 32
V_DIM = 64
D_MLA = MLA_HEADS * V_DIM
ROPE_BASE = 10000.0
D_MIX = D_CONV_A + D_SSD + D_MLA
NORM_EPS = 1e-6

LANES = 128
SUBLANES = 8
V7X_VMEM_BYTES = 64 * 1024 * 1024

SEQ_TILE = 512
SSD_CHUNK = 128
HEAD_SLAB = LANES
ROPE_HALF = QK_ROPE // 2
ROPE_ROW0 = QK_NOPE
ONES_ROW = V_DIM
V_SLAB = 80
KEY_BLOCK = 256
QUERY_BLOCK = 256
DIAG_LOOKAHEAD = 4
LOOP_LOOKAHEAD = 6
PROJ_GROUP = 512
DIAG_STAGES_PER_MIX_STAGE = 3
LOG2E = 1.4426950408889634
HALO = SUBLANES

OFF_A = 0
OFF_SZ = OFF_A + 4 * D_CONV_A
OFF_XBC = OFF_SZ + D_SSD
OFF_SDT = OFF_XBC + SSD_CONV_DIM
OFF_QA = OFF_SDT + SSD_HEADS
OFF_KV = OFF_QA + Q_LORA
OFF_KR = OFF_KV + KV_LORA
OFF_CZ = OFF_KR + QK_ROPE
IN_COLS = OFF_CZ + D_MLA
N_PROJ = -(-IN_COLS // LANES) * LANES

ROW_NORM_G = 0
ROW_CONV_A = 1
ROW_SSD_CONV = ROW_CONV_A + CONV_A_WIDTH
ROW_SSD_CONV_B = ROW_SSD_CONV + SSD_CONV_WIDTH
ROW_DT_BIAS = ROW_SSD_CONV_B + 1
ROW_A_LOG = ROW_DT_BIAS + 1
ROW_D_SKIP = ROW_A_LOG + 1
ROW_SSD_NORM_G = ROW_D_SKIP + 1
ROW_Q_NORM_G = ROW_SSD_NORM_G + 1
ROW_KV_NORM_G = ROW_Q_NORM_G + 1
ROW_FINAL_G = ROW_KV_NORM_G + 1
N_VEC_ROWS = 16


def _silu(v):
    return v * (1.0 / (1.0 + jnp.exp(-v)))


def _softplus(v):
    return jnp.maximum(v, 0.0) + jnp.log1p(jnp.exp(-jnp.abs(v)))


def _split3(v):
    hi = v.astype(BF16)
    r1 = v - hi.astype(F32)
    mid = r1.astype(BF16)
    lo = (r1 - mid.astype(F32)).astype(BF16)
    return hi, mid, lo


def _dot(a, b):
    return jnp.dot(a, b, preferred_element_type=F32)


def _dot_pieces(a, b):
    if isinstance(a, tuple):
        return functools.reduce(lambda acc, piece: acc + _dot(piece, b), a[1:], _dot(a[0], b))
    return functools.reduce(lambda acc, piece: acc + _dot(a, piece), b[1:], _dot(a, b[0]))


def _dot_nt(a, b):
    return lax.dot_general(a, b, (((1,), (1,)), ((), ())), preferred_element_type=F32)


def _interleave(*stage_iters):
    live = [it if isinstance(it, tuple) else (it, 1) for it in stage_iters]
    while live:
        for entry in list(live):
            it, per_round = entry
            for _ in range(per_round):
                if next(it, StopIteration) is StopIteration:
                    live.remove(entry)
                    break


def _layer_kernel(pos_ref, x_ref, invf_ref, vec_ref, w_in_ref, wqbt_ref, wk_ref,
                  wvt_ref, w_out_ref, o_ref,
                  ua_ref, xbc_ref, state_ref, kc_ref, vt_ref, qt_ref, acc_ref,
                  m_ref, yt_ref, ycat_ref, tril_ref, expand_ref, *proj_refs, final):
    j = pl.program_id(1)
    tile = x_ref.shape[0]

    def ptile(off, width):
        return jnp.concatenate([p[:, off:off + width] for p in proj_refs], axis=0)

    def vec(row, width):
        return vec_ref[row:row + 1, 0:width]

    L = SSD_CHUNK
    n_chunks = tile // L

    @pl.when(j == 0)
    def _():
        ua_ref[0:HALO, :] = jnp.zeros((HALO, D_CONV_A), F32)
        xbc_ref[0:HALO, :] = jnp.zeros((HALO, SSD_CONV_DIM), F32)
        state_ref[...] = jnp.zeros(state_ref.shape, F32)

    row_i = lax.broadcasted_iota(jnp.int32, (L, L), 0)
    col_i = lax.broadcasted_iota(jnp.int32, (L, L), 1)
    causal = row_i >= col_i
    tril_ref[...] = jnp.where(causal, 1.0, 0.0).astype(BF16)
    lane_lo = lax.broadcasted_iota(jnp.int32, (L, LANES), 1) < SSD_HEAD_DIM
    half = D_SSD // SSD_GROUPS
    grp0 = lax.broadcasted_iota(jnp.int32, (1, D_SSD), 1) < half
    grp0_state = lax.broadcasted_iota(jnp.int32, (1, SSD_BC), 1) < SSD_STATE
    head_lane = lax.broadcasted_iota(jnp.int32, (1, LANES), 1) < SSD_HEADS
    a_neg2 = -jnp.exp(vec(ROW_A_LOG, LANES)) * LOG2E
    head_of_lane = lax.shift_right_logical(lax.broadcasted_iota(jnp.int32, (LANES, D_SSD), 1),
                                           jnp.int32(SSD_HEAD_DIM.bit_length() - 1))
    expand_ref[...] = jnp.where(lax.broadcasted_iota(jnp.int32, (LANES, D_SSD), 0) == head_of_lane,
                                1.0, 0.0).astype(BF16)

    def project(c):
        rows = slice(c * L, (c + 1) * L)
        x = x_ref[rows, :]
        h = x * lax.rsqrt(jnp.mean(x * x, axis=-1, keepdims=True) + NORM_EPS) * vec(ROW_NORM_G, D_MODEL)
        hb = h.astype(BF16)
        yield
        for g0 in range(0, N_PROJ, PROJ_GROUP):
            g1 = min(g0 + PROJ_GROUP, N_PROJ)
            proj_refs[c][:, g0:g1] = _dot(hb, w_in_ref[:, g0:g1])
            yield

    def mix(c):
        rows = slice(c * L, (c + 1) * L)
        lo = HALO + c * L

        def pcol(off, width):
            return proj_refs[c][:, off:off + width]

        ua_ref[lo:lo + L, :] = pcol(OFF_A + 2 * D_CONV_A, D_CONV_A) * pcol(OFF_A, D_CONV_A)
        conv_a = vec(ROW_CONV_A + CONV_A_WIDTH - 1, D_CONV_A) * ua_ref[lo:lo + L, :]
        for k in range(1, CONV_A_WIDTH):
            conv_a += vec(ROW_CONV_A + CONV_A_WIDTH - 1 - k, D_CONV_A) * ua_ref[lo - k:lo - k + L, :]
        ycat_ref[rows, 0:D_CONV_A] = (pcol(OFF_A + D_CONV_A, D_CONV_A) * conv_a
                                      * _silu(pcol(OFF_A + 3 * D_CONV_A, D_CONV_A))).astype(BF16)
        yield

        xbc_ref[lo:lo + L, :] = pcol(OFF_XBC, SSD_CONV_DIM)
        conv_b = vec(ROW_SSD_CONV + SSD_CONV_WIDTH - 1, SSD_CONV_DIM) * xbc_ref[lo:lo + L, :]
        for k in range(1, SSD_CONV_WIDTH):
            conv_b += (vec(ROW_SSD_CONV + SSD_CONV_WIDTH - 1 - k, SSD_CONV_DIM)
                       * xbc_ref[lo - k:lo - k + L, :])
        xbc = _silu(conv_b + vec(ROW_SSD_CONV_B, SSD_CONV_DIM))
        xs = xbc[:, 0:D_SSD]
        bs = xbc[:, D_SSD:D_SSD + SSD_BC]
        cs = xbc[:, D_SSD + SSD_BC:SSD_CONV_DIM]
        yield
        dt_slab = _softplus(jnp.where(head_lane, pcol(OFF_SDT, LANES) + vec(ROW_DT_BIAS, LANES), 0.0))
        cum_slab = _dot_pieces(tril_ref[...], _split3(dt_slab * a_neg2))
        yield
        dt_c = _dot(dt_slab.astype(BF16), expand_ref[...])
        cum = _dot_pieces(_split3(cum_slab)[:2], expand_ref[...])
        tot = cum[L - 1:L, :]
        xd = xs * dt_c
        xd_b = xd.astype(BF16)
        bs_b = bs.astype(BF16)
        cs_b = cs.astype(BF16)
        state_b = state_ref[...].astype(BF16)
        zero_b = jnp.zeros((), BF16)

        bs_blocks = jnp.concatenate([jnp.where(grp0_state, bs_b, zero_b), jnp.where(grp0_state, zero_b, bs_b)], axis=0)
        cb_all = _dot_nt(cs_b, bs_blocks)
        yield
        y_diag_slabs = []
        for s in range(D_SSD // LANES):
            cum_slab = cum[:, s * LANES:(s + 1) * LANES]
            cum_rolled = pltpu.roll(cum_slab, SSD_HEAD_DIM, 1)
            cum_t = cum_slab.T
            pair = []
            for e in range(2):
                g = (2 * s + e) // (SSD_HEADS // SSD_GROUPS)
                col = jnp.where(lane_lo, cum_slab, cum_rolled) if e == 0 else jnp.where(lane_lo, cum_rolled, cum_slab)
                row = cum_t[e * SSD_HEAD_DIM:e * SSD_HEAD_DIM + 1, :]
                decay = jnp.exp2(jnp.where(causal, col - row, -jnp.inf))
                pair.append((cb_all[:, g * L:(g + 1) * L] * decay).astype(BF16))
            xd_slab = xd_b[:, s * LANES:(s + 1) * LANES]
            xd_pair = jnp.concatenate([jnp.where(lane_lo, xd_slab, zero_b), jnp.where(lane_lo, zero_b, xd_slab)], axis=0)
            y_diag_slabs.append(_dot(jnp.concatenate(pair, axis=1), xd_pair))
            yield
        y_diag = jnp.concatenate(y_diag_slabs, axis=1)

        state_blocks = jnp.concatenate([jnp.where(grp0, state_b, zero_b), jnp.where(grp0, zero_b, state_b)], axis=0)
        y_off = _dot(cs_b, state_blocks) * jnp.exp2(cum)

        xdd = (xd * jnp.exp2(tot - cum)).astype(BF16)
        xdd_blocks = jnp.concatenate([jnp.where(grp0, xdd, zero_b), jnp.where(grp0, zero_b, xdd)], axis=0)
        bs_t = jnp.concatenate([bs[:, g * SSD_STATE:(g + 1) * SSD_STATE].T for g in range(SSD_GROUPS)],
                               axis=1).astype(BF16)
        state_ref[...] = state_ref[...] * jnp.exp2(tot) + _dot(bs_t, xdd_blocks)
        yield

        gated = (y_diag + y_off + xs * vec(ROW_D_SKIP, D_SSD)) * _silu(pcol(OFF_SZ, D_SSD))
        g2 = gated * gated
        ss_all = jnp.sum(g2, axis=-1, keepdims=True)
        ss0 = jnp.sum(jnp.where(grp0, g2, 0.0), axis=-1, keepdims=True)
        inv0 = lax.rsqrt(ss0 * (1.0 / half) + SSD_NORM_EPS)
        inv1 = lax.rsqrt((ss_all - ss0) * (1.0 / half) + SSD_NORM_EPS)
        y_b = gated * jnp.where(grp0, inv0, inv1) * vec(ROW_SSD_NORM_G, D_SSD)
        ycat_ref[rows, D_CONV_A:D_CONV_A + D_SSD] = y_b.astype(BF16)
        yield

    scale = (QK_NOPE + QK_ROPE) ** -0.5 * LOG2E
    r0 = ROPE_ROW0
    kb = KEY_BLOCK

    def attend_prepare():
        ang = pos_ref[...].astype(F32) * invf_ref[...]
        cos_t = jnp.cos(ang)
        sin_t = jnp.sin(ang)

        def rope_rows(t1, t2):
            return t1 * cos_t - t2 * sin_t, t2 * cos_t + t1 * sin_t

        c_qa = ptile(OFF_QA, Q_LORA)
        qan = (c_qa * lax.rsqrt(jnp.mean(c_qa * c_qa, axis=-1, keepdims=True) + NORM_EPS)
               * vec(ROW_Q_NORM_G, Q_LORA)).astype(BF16)
        q_t = _dot_nt(wqbt_ref[...], qan)
        yield
        for hd in range(MLA_HEADS):
            base = hd * HEAD_SLAB
            q1, q2 = rope_rows(q_t[base + r0:base + r0 + ROPE_HALF, :],
                               q_t[base + r0 + ROPE_HALF:base + r0 + QK_ROPE, :])
            slab = jnp.concatenate([q_t[base:base + r0, :], q1, q2,
                                    q_t[base + r0 + QK_ROPE:base + HEAD_SLAB, :]], axis=0)
            qt_ref[base:base + HEAD_SLAB, :] = (slab * scale).astype(BF16)
        yield

        c_kv = ptile(OFF_KV, KV_LORA)
        kvn = (c_kv * lax.rsqrt(jnp.mean(c_kv * c_kv, axis=-1, keepdims=True) + NORM_EPS)
               * vec(ROW_KV_NORM_G, KV_LORA)).astype(BF16)
        kr_slab0 = OFF_KR // LANES * LANES
        kr_lane = lax.broadcasted_iota(jnp.int32, (tile, LANES), 1)
        kr_rolled = pltpu.roll(ptile(kr_slab0, LANES), r0 - (OFF_KR - kr_slab0), 1)
        ckr_t = jnp.where((kr_lane >= r0) & (kr_lane < r0 + QK_ROPE), kr_rolled, 0.0).T
        k1, k2 = rope_rows(ckr_t[r0:r0 + ROPE_HALF, :], ckr_t[r0 + ROPE_HALF:r0 + QK_ROPE, :])
        kr_nat = jnp.concatenate([ckr_t[0:r0, :], k1, k2, ckr_t[r0 + QK_ROPE:HEAD_SLAB, :]], axis=0).T
        yield
        k_nat = _dot(kvn, wk_ref[...])
        v_t = _dot_nt(wvt_ref[...], kvn)
        ones_row = lax.broadcasted_iota(jnp.int32, (V_SLAB, tile), 0) == ONES_ROW
        for hd in range(MLA_HEADS):
            base = hd * HEAD_SLAB
            kc_ref[j, :, base:base + HEAD_SLAB] = (k_nat[:, base:base + HEAD_SLAB] + kr_nat).astype(BF16)
            vt_ref[j, hd * V_SLAB:(hd + 1) * V_SLAB, :] = jnp.where(
                ones_row, 1.0, v_t[hd * V_SLAB:(hd + 1) * V_SLAB, :]).astype(BF16)
        m_ref[...] = jnp.full(m_ref.shape, -jnp.inf, F32)
        acc_ref[...] = jnp.zeros(acc_ref.shape, F32)
        yield

    def attend(tiles, masked, lookahead):
        def spans(sub):
            start = sub * kb if masked else 0
            cuts = [start] + [q for q in range(0, tile + 1, QUERY_BLOCK) if q > start]
            return list(zip(cuts[:-1], cuts[1:]))

        def scores(i, sub, q0, q1, hd):
            base = hd * HEAD_SLAB
            return _dot(kc_ref[i, sub * kb:(sub + 1) * kb, base:base + HEAD_SLAB],
                        qt_ref[base:base + HEAD_SLAB, q0:q1])

        steps = [(i, sub, q0, q1, hd) for i in tiles for sub in range(tile // kb)
                 for q0, q1 in spans(sub) for hd in range(MLA_HEADS)]
        pending = [scores(*st) for st in steps[:lookahead]]
        for n, (i, sub, q0, q1, hd) in enumerate(steps):
            s_t = pending.pop(0)
            if n + lookahead < len(steps):
                pending.append(scores(*steps[n + lookahead]))
            vrows = slice(hd * V_SLAB, (hd + 1) * V_SLAB)
            if masked and q0 < (sub + 1) * kb:
                visible = (lax.broadcasted_iota(jnp.int32, s_t.shape, 0) + (sub * kb - q0)
                           <= lax.broadcasted_iota(jnp.int32, s_t.shape, 1))
                s_t = jnp.where(visible, s_t, -jnp.inf)
            m_old = m_ref[hd, :, q0:q1]
            m_new = jnp.maximum(m_old, jnp.max(s_t, axis=0, keepdims=True))
            p = jnp.exp2(s_t - m_new).astype(BF16)
            alpha = jnp.exp2(m_old - m_new)
            acc_ref[vrows, q0:q1] = (alpha * acc_ref[vrows, q0:q1]
                                     + _dot(vt_ref[i, vrows, sub * kb:(sub + 1) * kb], p))
            m_ref[hd, :, q0:q1] = m_new
            yield

    def attend_diagonal():
        yield from attend_prepare()
        yield from attend([j], True, DIAG_LOOKAHEAD)

    d_ab = D_CONV_A + D_SSD

    def out_partial(chunks):
        for c in chunks:
            rows = slice(c * L, (c + 1) * L)
            o_ref[rows, :] = x_ref[rows, :] + _dot(ycat_ref[rows, 0:d_ab], w_out_ref[0:d_ab, :])
            yield

    _interleave(project(0))
    for c in range(n_chunks - 1):
        _interleave(project(c + 1), mix(c))
    _interleave((attend_diagonal(), DIAG_STAGES_PER_MIX_STAGE), mix(n_chunks - 1),
                out_partial(range(n_chunks - 1)))
    _interleave(out_partial([n_chunks - 1]))
    ua_ref[0:HALO, :] = ua_ref[tile:tile + HALO, :]
    xbc_ref[0:HALO, :] = xbc_ref[tile:tile + HALO, :]

    def pair_body(p, carry):
        _interleave(attend([2 * p, 2 * p + 1], False, LOOP_LOOKAHEAD))
        return carry

    lax.fori_loop(0, lax.shift_right_logical(j, 1), pair_body, 0)

    @pl.when(lax.rem(j, 2) == 1)
    def _():
        _interleave(attend([j - 1], False, LOOP_LOOKAHEAD))

    for hd in range(MLA_HEADS):
        base = hd * V_SLAB
        yt_ref[hd * V_DIM:(hd + 1) * V_DIM, :] = (acc_ref[base:base + V_DIM, :]
                                                  / acc_ref[base + ONES_ROW:base + ONES_ROW + 1, :])
    c_z = ptile(OFF_CZ, D_MLA)
    ycat_ref[:, D_CONV_A + D_SSD:D_MIX] = (yt_ref[...].T * _silu(c_z)).astype(BF16)

    out = o_ref[...] + _dot(ycat_ref[:, d_ab:D_MIX], w_out_ref[d_ab:D_MIX, :])
    if final:
        out = (out * lax.rsqrt(jnp.mean(out * out, axis=-1, keepdims=True) + NORM_EPS)
               * vec(ROW_FINAL_G, D_MODEL))
    o_ref[...] = out


def _prep_layer(norm_g, conv_a_w, ssd_conv_w, ssd_conv_b, ssd_dt_bias, ssd_a_log, ssd_d,
                ssd_norm_g, mla_q_norm_g, w_qb, mla_kv_norm_g, w_kvb, final_norm_g):
    wqb = w_qb.reshape(Q_LORA, MLA_HEADS, QK_NOPE + QK_ROPE)
    wqbt = jnp.pad(wqb, ((0, 0), (0, 0), (0, HEAD_SLAB - QK_NOPE - QK_ROPE))).reshape(
        Q_LORA, MLA_HEADS * HEAD_SLAB).T.astype(BF16)
    wkv = w_kvb.reshape(KV_LORA, MLA_HEADS, QK_NOPE + V_DIM)
    wk = jnp.pad(wkv[:, :, :QK_NOPE], ((0, 0), (0, 0), (0, HEAD_SLAB - QK_NOPE))).reshape(
        KV_LORA, MLA_HEADS * HEAD_SLAB).astype(BF16)
    wvt = jnp.pad(wkv[:, :, QK_NOPE:], ((0, 0), (0, 0), (0, V_SLAB - V_DIM))).reshape(
        KV_LORA, MLA_HEADS * V_SLAB).T.astype(BF16)

    def row(v):
        return jnp.pad(v.astype(F32), (0, D_MODEL - v.shape[0]))[None, :]

    rows = [row(norm_g)]
    rows += [row(conv_a_w[k]) for k in range(CONV_A_WIDTH)]
    rows += [row(ssd_conv_w[k]) for k in range(SSD_CONV_WIDTH)]
    rows += [row(ssd_conv_b), row(ssd_dt_bias), row(ssd_a_log), row(jnp.repeat(ssd_d, SSD_HEAD_DIM)),
             row(ssd_norm_g), row(mla_q_norm_g), row(mla_kv_norm_g), row(final_norm_g)]
    rows += [jnp.zeros((N_VEC_ROWS - len(rows), D_MODEL), F32)]
    vecs = jnp.concatenate(rows, axis=0)
    return vecs, wqbt, wk, wvt


def _vmem_limit(tile, n_tiles):
    f32, bf16 = 4, 2
    blocks = 2 * (2 * tile * D_MODEL * f32)
    weights = 2 * bf16 * (D_MODEL * N_PROJ + Q_LORA * MLA_HEADS * HEAD_SLAB
                          + KV_LORA * MLA_HEADS * (HEAD_SLAB + V_SLAB) + D_MIX * D_MODEL)
    scratch = (tile * N_PROJ * f32 + (tile + HALO) * (D_CONV_A + SSD_CONV_DIM) * f32
               + SSD_STATE * D_SSD * f32 + n_tiles * tile * MLA_HEADS * (HEAD_SLAB + V_SLAB) * bf16
               + MLA_HEADS * HEAD_SLAB * tile * bf16 + MLA_HEADS * V_SLAB * tile * f32
               + D_MLA * tile * f32 + tile * D_MIX * bf16)
    temporaries = 2 * tile * N_PROJ * f32
    return min(blocks + weights + scratch + temporaries, V7X_VMEM_BYTES - 8 * 1024 * 1024)


def _layer_call(x, pos3, invf, params, w_in_all, w_out_all, *, layer, final):
    batch, seq, d_model = x.shape
    tile = SEQ_TILE
    n_tiles = seq // tile
    vecs, wqbt, wk, wvt = params
    const = lambda shape: pl.BlockSpec(shape, lambda b, j: (0,) * len(shape))
    of_layer = lambda a: pl.BlockSpec((None,) + a.shape[1:], lambda b, j: (layer, 0, 0))
    return pl.pallas_call(
        functools.partial(_layer_kernel, final=final),
        out_shape=jax.ShapeDtypeStruct(x.shape, F32),
        grid=(batch, n_tiles),
        in_specs=[
            pl.BlockSpec((None, 1, tile), lambda b, j: (b, 0, j)),
            pl.BlockSpec((None, tile, d_model), lambda b, j: (b, j, 0)),
            const(invf.shape), const(vecs.shape), of_layer(w_in_all),
            const(wqbt.shape), const(wk.shape), const(wvt.shape), of_layer(w_out_all),
        ],
        out_specs=pl.BlockSpec((None, tile, d_model), lambda b, j: (b, j, 0)),
        scratch_shapes=[
            pltpu.VMEM((tile + HALO, D_CONV_A), F32),
            pltpu.VMEM((tile + HALO, SSD_CONV_DIM), F32),
            pltpu.VMEM((SSD_STATE, D_SSD), F32),
            pltpu.VMEM((n_tiles, tile, MLA_HEADS * HEAD_SLAB), BF16),
            pltpu.VMEM((n_tiles, MLA_HEADS * V_SLAB, tile), BF16),
            pltpu.VMEM((MLA_HEADS * HEAD_SLAB, tile), BF16),
            pltpu.VMEM((MLA_HEADS * V_SLAB, tile), F32),
            pltpu.VMEM((MLA_HEADS, 1, tile), F32),
            pltpu.VMEM((D_MLA, tile), F32),
            pltpu.VMEM((tile, D_MIX), BF16),
            pltpu.VMEM((SSD_CHUNK, SSD_CHUNK), BF16),
            pltpu.VMEM((LANES, D_SSD), BF16),
        ] + [pltpu.VMEM((SSD_CHUNK, N_PROJ), F32)] * (tile // SSD_CHUNK),
        compiler_params=pltpu.CompilerParams(
            dimension_semantics=("arbitrary", "arbitrary"),
            vmem_limit_bytes=_vmem_limit(tile, n_tiles)),
        name="hybrid_layer_final" if final else "hybrid_layer",
    )(pos3, x, invf, vecs, w_in_all, wqbt, wk, wvt, w_out_all)


def kernel(x, positions, norm_g, w_in, conv_a_w, ssd_conv_w, ssd_conv_b, ssd_dt_bias, ssd_a_log, ssd_d,
           ssd_norm_g, mla_q_norm_g, w_qb, mla_kv_norm_g, w_kvb, w_out, final_norm_g):
    batch, seq, _ = x.shape
    depth = norm_g.shape[0]
    assert seq % SEQ_TILE == 0 and SEQ_TILE % SSD_CHUNK == 0 and SEQ_TILE % KEY_BLOCK == 0
    inv_freq = ROPE_BASE ** (-jnp.arange(0, QK_ROPE, 2, dtype=F32) / QK_ROPE)
    invf = jnp.broadcast_to(inv_freq[:, None], (ROPE_HALF, SEQ_TILE))
    pos3 = positions.reshape(batch, 1, seq)
    w_in_all = jnp.pad(w_in.astype(BF16), ((0, 0), (0, 0), (0, N_PROJ - IN_COLS)))
    w_out_all = w_out.astype(BF16)
    for l in range(depth):
        params = _prep_layer(norm_g[l], conv_a_w[l], ssd_conv_w[l], ssd_conv_b[l], ssd_dt_bias[l],
                             ssd_a_log[l], ssd_d[l], ssd_norm_g[l], mla_q_norm_g[l], w_qb[l],
                             mla_kv_norm_g[l], w_kvb[l], final_norm_g)
        x = _layer_call(x, pos3, invf, params, w_in_all, w_out_all, layer=l, final=(l == depth - 1))
    return x
```

```python
import functools

import jax
import jax.numpy as jnp
from jax import lax
from jax.experimental import pallas as pl
from jax.experimental.pallas import tpu as pltpu

F32 = jnp.float32
BF16 = jnp.bfloat16

D_MODEL = 1024
D_CONV_A = 256
CONV_A_WIDTH = 3
SSD_HEADS = 6
SSD_HEAD_DIM = 64
D_SSD = SSD_HEADS * SSD_HEAD_DIM
SSD_GROUPS = 2
SSD_STATE = 128
SSD_CONV_WIDTH = 4
SSD_BC = SSD_GROUPS * SSD_STATE
SSD_CONV_DIM = D_SSD + 2 * SSD_BC
SSD_NORM_EPS = 1e-5
MLA_HEADS = 6
Q_LORA = 256
KV_LORA = 128
QK_NOPE = 64
QK_ROPE = 32
V_DIM = 64
D_MLA = MLA_HEADS * V_DIM
ROPE_BASE = 10000.0
D_MIX = D_CONV_A + D_SSD + D_MLA
NORM_EPS = 1e-6

LANES = 128
SUBLANES = 8
V7X_VMEM_BYTES = 64 * 1024 * 1024

SEQ_TILE = 512
SSD_CHUNK = 128
HEAD_SLAB = LANES
ROPE_HALF = QK_ROPE // 2
ROPE_ROW0 = QK_NOPE
ONES_ROW = V_DIM
V_SLAB = 80
KEY_BLOCK = 256
QUERY_BLOCK = 256
DIAG_LOOKAHEAD = 4
LOOP_LOOKAHEAD = 6
UNSCALED_LOOKAHEAD = 3
PROJ_GROUP = 512
DIAG_STAGES_PER_MIX_STAGE = 3
LOG2E = 1.4426950408889634
UNSCALED_EXCESS_MAX = 60.0
HALO = SUBLANES

OFF_A = 0
OFF_SZ = OFF_A + 4 * D_CONV_A
OFF_XBC = OFF_SZ + D_SSD
OFF_SDT = OFF_XBC + SSD_CONV_DIM
OFF_QA = OFF_SDT + SSD_HEADS
OFF_KV = OFF_QA + Q_LORA
OFF_KR = OFF_KV + KV_LORA
OFF_CZ = OFF_KR + QK_ROPE
IN_COLS = OFF_CZ + D_MLA
N_PROJ = -(-IN_COLS // LANES) * LANES

ROW_NORM_G = 0
ROW_CONV_A = 1
ROW_SSD_CONV = ROW_CONV_A + CONV_A_WIDTH
ROW_SSD_CONV_B = ROW_SSD_CONV + SSD_CONV_WIDTH
ROW_DT_BIAS = ROW_SSD_CONV_B + 1
ROW_A_LOG = ROW_DT_BIAS + 1
ROW_D_SKIP = ROW_A_LOG + 1
ROW_SSD_NORM_G = ROW_D_SKIP + 1
ROW_Q_NORM_G = ROW_SSD_NORM_G + 1
ROW_KV_NORM_G = ROW_Q_NORM_G + 1
ROW_FINAL_G = ROW_KV_NORM_G + 1
N_VEC_ROWS = 16


def _silu(v):
    return v * (1.0 / (1.0 + jnp.exp(-v)))


def _softplus(v):
    return jnp.maximum(v, 0.0) + jnp.log1p(jnp.exp(-jnp.abs(v)))


def _split3(v):
    hi = v.astype(BF16)
    r1 = v - hi.astype(F32)
    mid = r1.astype(BF16)
    lo = (r1 - mid.astype(F32)).astype(BF16)
    return hi, mid, lo


def _dot(a, b):
    return jnp.dot(a, b, preferred_element_type=F32)


def _dot_pieces(a, b):
    if isinstance(a, tuple):
        return functools.reduce(lambda acc, piece: acc + _dot(piece, b), a[1:], _dot(a[0], b))
    return functools.reduce(lambda acc, piece: acc + _dot(a, piece), b[1:], _dot(a, b[0]))


def _dot_nt(a, b):
    return lax.dot_general(a, b, (((1,), (1,)), ((), ())), preferred_element_type=F32)


def _interleave(*stage_iters):
    live = [it if isinstance(it, tuple) else (it, 1) for it in stage_iters]
    while live:
        for entry in list(live):
            it, per_round = entry
            for _ in range(per_round):
                if next(it, StopIteration) is StopIteration:
                    live.remove(entry)
                    break


def _layer_kernel(pos_ref, x_ref, invf_ref, vec_ref, w_in_ref, wqbt_ref, wk_ref,
                  wvt_ref, w_out_ref, o_ref,
                  ua_ref, xbc_ref, state_ref, kc_ref, vt_ref, qt_ref, acc_ref,
                  m_ref, pv_ref, cmax_ref, yt_ref, ycat_ref, tril_ref, expand_ref, *proj_refs, final):
    j = pl.program_id(1)
    tile = x_ref.shape[0]

    def ptile(off, width):
        return jnp.concatenate([p[:, off:off + width] for p in proj_refs], axis=0)

    def vec(row, width):
        return vec_ref[row:row + 1, 0:width]

    L = SSD_CHUNK
    n_chunks = tile // L

    @pl.when(j == 0)
    def _():
        ua_ref[0:HALO, :] = jnp.zeros((HALO, D_CONV_A), F32)
        xbc_ref[0:HALO, :] = jnp.zeros((HALO, SSD_CONV_DIM), F32)
        state_ref[...] = jnp.zeros(state_ref.shape, F32)

    row_i = lax.broadcasted_iota(jnp.int32, (L, L), 0)
    col_i = lax.broadcasted_iota(jnp.int32, (L, L), 1)
    causal = row_i >= col_i
    tril_ref[...] = jnp.where(causal, 1.0, 0.0).astype(BF16)
    lane_lo = lax.broadcasted_iota(jnp.int32, (L, LANES), 1) < SSD_HEAD_DIM
    half = D_SSD // SSD_GROUPS
    grp0 = lax.broadcasted_iota(jnp.int32, (1, D_SSD), 1) < half
    grp0_state = lax.broadcasted_iota(jnp.int32, (1, SSD_BC), 1) < SSD_STATE
    head_lane = lax.broadcasted_iota(jnp.int32, (1, LANES), 1) < SSD_HEADS
    a_neg2 = -jnp.exp(vec(ROW_A_LOG, LANES)) * LOG2E
    head_of_lane = lax.shift_right_logical(lax.broadcasted_iota(jnp.int32, (LANES, D_SSD), 1),
                                           jnp.int32(SSD_HEAD_DIM.bit_length() - 1))
    expand_ref[...] = jnp.where(lax.broadcasted_iota(jnp.int32, (LANES, D_SSD), 0) == head_of_lane,
                                1.0, 0.0).astype(BF16)

    def project(c):
        rows = slice(c * L, (c + 1) * L)
        x = x_ref[rows, :]
        h = x * lax.rsqrt(jnp.mean(x * x, axis=-1, keepdims=True) + NORM_EPS) * vec(ROW_NORM_G, D_MODEL)
        hb = h.astype(BF16)
        yield
        for g0 in range(0, N_PROJ, PROJ_GROUP):
            g1 = min(g0 + PROJ_GROUP, N_PROJ)
            proj_refs[c][:, g0:g1] = _dot(hb, w_in_ref[:, g0:g1])
            yield

    def mix(c):
        rows = slice(c * L, (c + 1) * L)
        lo = HALO + c * L

        def pcol(off, width):
            return proj_refs[c][:, off:off + width]

        ua_ref[lo:lo + L, :] = pcol(OFF_A + 2 * D_CONV_A, D_CONV_A) * pcol(OFF_A, D_CONV_A)
        conv_a = vec(ROW_CONV_A + CONV_A_WIDTH - 1, D_CONV_A) * ua_ref[lo:lo + L, :]
        for k in range(1, CONV_A_WIDTH):
            conv_a += vec(ROW_CONV_A + CONV_A_WIDTH - 1 - k, D_CONV_A) * ua_ref[lo - k:lo - k + L, :]
        ycat_ref[rows, 0:D_CONV_A] = (pcol(OFF_A + D_CONV_A, D_CONV_A) * conv_a
                                      * _silu(pcol(OFF_A + 3 * D_CONV_A, D_CONV_A))).astype(BF16)
        yield

        xbc_ref[lo:lo + L, :] = pcol(OFF_XBC, SSD_CONV_DIM)
        conv_b = vec(ROW_SSD_CONV + SSD_CONV_WIDTH - 1, SSD_CONV_DIM) * xbc_ref[lo:lo + L, :]
        for k in range(1, SSD_CONV_WIDTH):
            conv_b += (vec(ROW_SSD_CONV + SSD_CONV_WIDTH - 1 - k, SSD_CONV_DIM)
                       * xbc_ref[lo - k:lo - k + L, :])
        xbc = _silu(conv_b + vec(ROW_SSD_CONV_B, SSD_CONV_DIM))
        xs = xbc[:, 0:D_SSD]
        bs = xbc[:, D_SSD:D_SSD + SSD_BC]
        cs = xbc[:, D_SSD + SSD_BC:SSD_CONV_DIM]
        yield
        dt_slab = _softplus(jnp.where(head_lane, pcol(OFF_SDT, LANES) + vec(ROW_DT_BIAS, LANES), 0.0))
        cum_slab = _dot_pieces(tril_ref[...], _split3(dt_slab * a_neg2))
        yield
        dt_c = _dot(dt_slab.astype(BF16), expand_ref[...])
        cum = _dot_pieces(_split3(cum_slab)[:2], expand_ref[...])
        tot = cum[L - 1:L, :]
        xd = xs * dt_c
        xd_b = xd.astype(BF16)
        bs_b = bs.astype(BF16)
        cs_b = cs.astype(BF16)
        state_b = state_ref[...].astype(BF16)
        zero_b = jnp.zeros((), BF16)

        bs_blocks = jnp.concatenate([jnp.where(grp0_state, bs_b, zero_b), jnp.where(grp0_state, zero_b, bs_b)], axis=0)
        cb_all = _dot_nt(cs_b, bs_blocks)
        yield
        y_diag_slabs = []
        for s in range(D_SSD // LANES):
            cum_slab = cum[:, s * LANES:(s + 1) * LANES]
            cum_rolled = pltpu.roll(cum_slab, SSD_HEAD_DIM, 1)
            cum_t = cum_slab.T
            pair = []
            for e in range(2):
                g = (2 * s + e) // (SSD_HEADS // SSD_GROUPS)
                col = jnp.where(lane_lo, cum_slab, cum_rolled) if e == 0 else jnp.where(lane_lo, cum_rolled, cum_slab)
                row = cum_t[e * SSD_HEAD_DIM:e * SSD_HEAD_DIM + 1, :]
                decay = jnp.exp2(jnp.where(causal, col - row, -jnp.inf))
                pair.append((cb_all[:, g * L:(g + 1) * L] * decay).astype(BF16))
            xd_slab = xd_b[:, s * LANES:(s + 1) * LANES]
            xd_pair = jnp.concatenate([jnp.where(lane_lo, xd_slab, zero_b), jnp.where(lane_lo, zero_b, xd_slab)], axis=0)
            y_diag_slabs.append(_dot(jnp.concatenate(pair, axis=1), xd_pair))
            yield
        y_diag = jnp.concatenate(y_diag_slabs, axis=1)

        state_blocks = jnp.concatenate([jnp.where(grp0, state_b, zero_b), jnp.where(grp0, zero_b, state_b)], axis=0)
        y_off = _dot(cs_b, state_blocks) * jnp.exp2(cum)

        xdd = (xd * jnp.exp2(tot - cum)).astype(BF16)
        xdd_blocks = jnp.concatenate([jnp.where(grp0, xdd, zero_b), jnp.where(grp0, zero_b, xdd)], axis=0)
        bs_t = jnp.concatenate([bs[:, g * SSD_STATE:(g + 1) * SSD_STATE].T for g in range(SSD_GROUPS)],
                               axis=1).astype(BF16)
        state_ref[...] = state_ref[...] * jnp.exp2(tot) + _dot(bs_t, xdd_blocks)
        yield

        gated = (y_diag + y_off + xs * vec(ROW_D_SKIP, D_SSD)) * _silu(pcol(OFF_SZ, D_SSD))
        g2 = gated * gated
        ss_all = jnp.sum(g2, axis=-1, keepdims=True)
        ss0 = jnp.sum(jnp.where(grp0, g2, 0.0), axis=-1, keepdims=True)
        inv0 = lax.rsqrt(ss0 * (1.0 / half) + SSD_NORM_EPS)
        inv1 = lax.rsqrt((ss_all - ss0) * (1.0 / half) + SSD_NORM_EPS)
        y_b = gated * jnp.where(grp0, inv0, inv1) * vec(ROW_SSD_NORM_G, D_SSD)
        ycat_ref[rows, D_CONV_A:D_CONV_A + D_SSD] = y_b.astype(BF16)
        yield

    scale = (QK_NOPE + QK_ROPE) ** -0.5 * LOG2E
    r0 = ROPE_ROW0
    kb = KEY_BLOCK

    def attend_prepare():
        ang = pos_ref[...].astype(F32) * invf_ref[...]
        cos_t = jnp.cos(ang)
        sin_t = jnp.sin(ang)

        def rope_rows(t1, t2):
            return t1 * cos_t - t2 * sin_t, t2 * cos_t + t1 * sin_t

        c_qa = ptile(OFF_QA, Q_LORA)
        qan = (c_qa * lax.rsqrt(jnp.mean(c_qa * c_qa, axis=-1, keepdims=True) + NORM_EPS)
               * vec(ROW_Q_NORM_G, Q_LORA)).astype(BF16)
        q_t = _dot_nt(wqbt_ref[...], qan)
        yield
        for hd in range(MLA_HEADS):
            base = hd * HEAD_SLAB
            q1, q2 = rope_rows(q_t[base + r0:base + r0 + ROPE_HALF, :],
                               q_t[base + r0 + ROPE_HALF:base + r0 + QK_ROPE, :])
            slab = jnp.concatenate([q_t[base:base + r0, :], q1, q2,
                                    q_t[base + r0 + QK_ROPE:base + HEAD_SLAB, :]], axis=0)
            qt_ref[base:base + HEAD_SLAB, :] = (slab * scale).astype(BF16)
        yield

        c_kv = ptile(OFF_KV, KV_LORA)
        kvn = (c_kv * lax.rsqrt(jnp.mean(c_kv * c_kv, axis=-1, keepdims=True) + NORM_EPS)
               * vec(ROW_KV_NORM_G, KV_LORA)).astype(BF16)
        kr_slab0 = OFF_KR // LANES * LANES
        kr_lane = lax.broadcasted_iota(jnp.int32, (tile, LANES), 1)
        kr_rolled = pltpu.roll(ptile(kr_slab0, LANES), r0 - (OFF_KR - kr_slab0), 1)
        ckr_t = jnp.where((kr_lane >= r0) & (kr_lane < r0 + QK_ROPE), kr_rolled, 0.0).T
        k1, k2 = rope_rows(ckr_t[r0:r0 + ROPE_HALF, :], ckr_t[r0 + ROPE_HALF:r0 + QK_ROPE, :])
        kr_nat = jnp.concatenate([ckr_t[0:r0, :], k1, k2, ckr_t[r0 + QK_ROPE:HEAD_SLAB, :]], axis=0).T
        yield
        k_nat = _dot(kvn, wk_ref[...])
        v_t = _dot_nt(wvt_ref[...], kvn)
        ones_row = lax.broadcasted_iota(jnp.int32, (V_SLAB, tile), 0) == ONES_ROW
        for hd in range(MLA_HEADS):
            base = hd * HEAD_SLAB
            kc_ref[j, :, base:base + HEAD_SLAB] = (k_nat[:, base:base + HEAD_SLAB] + kr_nat).astype(BF16)
            vt_ref[j, hd * V_SLAB:(hd + 1) * V_SLAB, :] = jnp.where(
                ones_row, 1.0, v_t[hd * V_SLAB:(hd + 1) * V_SLAB, :]).astype(BF16)
        m_ref[...] = jnp.full(m_ref.shape, -jnp.inf, F32)
        acc_ref[...] = jnp.zeros(acc_ref.shape, F32)
        yield

    def attend(tiles, masked, lookahead):
        def spans(sub):
            start = sub * kb if masked else 0
            cuts = [start] + [q for q in range(0, tile + 1, QUERY_BLOCK) if q > start]
            return list(zip(cuts[:-1], cuts[1:]))

        def scores(i, sub, q0, q1, hd):
            base = hd * HEAD_SLAB
            return _dot(kc_ref[i, sub * kb:(sub + 1) * kb, base:base + HEAD_SLAB],
                        qt_ref[base:base + HEAD_SLAB, q0:q1])

        steps = [(i, sub, q0, q1, hd) for i in tiles for sub in range(tile // kb)
                 for q0, q1 in spans(sub) for hd in range(MLA_HEADS)]
        pending = [scores(*st) for st in steps[:lookahead]]
        for n, (i, sub, q0, q1, hd) in enumerate(steps):
            s_t = pending.pop(0)
            if n + lookahead < len(steps):
                pending.append(scores(*steps[n + lookahead]))
            vrows = slice(hd * V_SLAB, (hd + 1) * V_SLAB)
            if masked and q0 < (sub + 1) * kb:
                visible = (lax.broadcasted_iota(jnp.int32, s_t.shape, 0) + (sub * kb - q0)
                           <= lax.broadcasted_iota(jnp.int32, s_t.shape, 1))
                s_t = jnp.where(visible, s_t, -jnp.inf)
            m_old = m_ref[hd, :, q0:q1]
            m_new = jnp.maximum(m_old, jnp.max(s_t, axis=0, keepdims=True))
            p = jnp.exp2(s_t - m_new).astype(BF16)
            alpha = jnp.exp2(m_old - m_new)
            acc_ref[vrows, q0:q1] = (alpha * acc_ref[vrows, q0:q1]
                                     + _dot(vt_ref[i, vrows, sub * kb:(sub + 1) * kb], p))
            m_ref[hd, :, q0:q1] = m_new
            yield

    def attend_diagonal():
        yield from attend_prepare()
        yield from attend([j], True, DIAG_LOOKAHEAD)

    d_ab = D_CONV_A + D_SSD

    def out_partial(chunks):
        for c in chunks:
            rows = slice(c * L, (c + 1) * L)
            o_ref[rows, :] = x_ref[rows, :] + _dot(ycat_ref[rows, 0:d_ab], w_out_ref[0:d_ab, :])
            yield

    _interleave(project(0))
    for c in range(n_chunks - 1):
        _interleave(project(c + 1), mix(c))
    _interleave((attend_diagonal(), DIAG_STAGES_PER_MIX_STAGE), mix(n_chunks - 1),
                out_partial(range(n_chunks - 1)))
    _interleave(out_partial([n_chunks - 1]))
    ua_ref[0:HALO, :] = ua_ref[tile:tile + HALO, :]
    xbc_ref[0:HALO, :] = xbc_ref[tile:tile + HALO, :]

    def visit_unscaled(tiles):
        blocks = [(i, sub) for i in tiles for sub in range(tile // kb)]
        groups = [(q0, hd) for q0 in range(0, tile, QUERY_BLOCK) for hd in range(MLA_HEADS)]
        steps = [(q0, hd, n_blk) for q0, hd in groups for n_blk in range(len(blocks))]

        def scores(q0, hd, n_blk):
            base = hd * HEAD_SLAB
            i, sub = blocks[n_blk]
            return _dot(kc_ref[i, sub * kb:(sub + 1) * kb, base:base + HEAD_SLAB],
                        qt_ref[base:base + HEAD_SLAB, q0:q0 + QUERY_BLOCK])

        pending = [scores(*st) for st in steps[:UNSCALED_LOOKAHEAD]]
        excess = None
        for n, (q0, hd, n_blk) in enumerate(steps):
            s_t = pending.pop(0)
            if n + UNSCALED_LOOKAHEAD < len(steps):
                pending.append(scores(*steps[n + UNSCALED_LOOKAHEAD]))
            vrows = slice(hd * V_SLAB, (hd + 1) * V_SLAB)
            i, sub = blocks[n_blk]
            first, last = n_blk == 0, n_blk == len(blocks) - 1
            if first:
                m_row = m_ref[hd, :, q0:q0 + QUERY_BLOCK]
            p = jnp.exp2(s_t - m_row).astype(BF16)
            pv_step = _dot(vt_ref[i, vrows, sub * kb:(sub + 1) * kb], p)
            cmax_step = jnp.max(s_t, axis=0, keepdims=True)
            pv = pv_step if first else pv + pv_step
            cmax = cmax_step if first else jnp.maximum(cmax, cmax_step)
            if last:
                pv_ref[vrows, q0:q0 + QUERY_BLOCK] = pv
                cmax_ref[hd, :, q0:q0 + QUERY_BLOCK] = cmax
                over = cmax - m_row
                excess = over if excess is None else jnp.maximum(excess, over)
        return jnp.max(excess)

    def commit_unscaled():
        for q0 in range(0, tile, QUERY_BLOCK):
            for hd in range(MLA_HEADS):
                vrows = slice(hd * V_SLAB, (hd + 1) * V_SLAB)
                m_row = m_ref[hd, :, q0:q0 + QUERY_BLOCK]
                m_new = jnp.maximum(m_row, cmax_ref[hd, :, q0:q0 + QUERY_BLOCK])
                acc_ref[vrows, q0:q0 + QUERY_BLOCK] = jnp.exp2(m_row - m_new) * (
                    acc_ref[vrows, q0:q0 + QUERY_BLOCK] + pv_ref[vrows, q0:q0 + QUERY_BLOCK])
                m_ref[hd, :, q0:q0 + QUERY_BLOCK] = m_new

    def visit(tiles):
        safe = visit_unscaled(tiles) <= UNSCALED_EXCESS_MAX
        pl.when(safe)(commit_unscaled)
        pl.when(jnp.logical_not(safe))(lambda: _interleave(attend(tiles, False, LOOP_LOOKAHEAD)))

    def pair_body(p, carry):
        visit([2 * p, 2 * p + 1])
        return carry

    lax.fori_loop(0, lax.shift_right_logical(j, 1), pair_body, 0)

    @pl.when(lax.rem(j, 2) == 1)
    def _():
        visit([j - 1])

    for hd in range(MLA_HEADS):
        base = hd * V_SLAB
        yt_ref[hd * V_DIM:(hd + 1) * V_DIM, :] = (acc_ref[base:base + V_DIM, :]
                                                  / acc_ref[base + ONES_ROW:base + ONES_ROW + 1, :])
    c_z = ptile(OFF_CZ, D_MLA)
    ycat_ref[:, D_CONV_A + D_SSD:D_MIX] = (yt_ref[...].T * _silu(c_z)).astype(BF16)

    out = o_ref[...] + _dot(ycat_ref[:, d_ab:D_MIX], w_out_ref[d_ab:D_MIX, :])
    if final:
        out = (out * lax.rsqrt(jnp.mean(out * out, axis=-1, keepdims=True) + NORM_EPS)
               * vec(ROW_FINAL_G, D_MODEL))
    o_ref[...] = out


def _prep_layer(norm_g, conv_a_w, ssd_conv_w, ssd_conv_b, ssd_dt_bias, ssd_a_log, ssd_d,
                ssd_norm_g, mla_q_norm_g, w_qb, mla_kv_norm_g, w_kvb, final_norm_g):
    wqb = w_qb.reshape(Q_LORA, MLA_HEADS, QK_NOPE + QK_ROPE)
    wqbt = jnp.pad(wqb, ((0, 0), (0, 0), (0, HEAD_SLAB - QK_NOPE - QK_ROPE))).reshape(
        Q_LORA, MLA_HEADS * HEAD_SLAB).T.astype(BF16)
    wkv = w_kvb.reshape(KV_LORA, MLA_HEADS, QK_NOPE + V_DIM)
    wk = jnp.pad(wkv[:, :, :QK_NOPE], ((0, 0), (0, 0), (0, HEAD_SLAB - QK_NOPE))).reshape(
        KV_LORA, MLA_HEADS * HEAD_SLAB).astype(BF16)
    wvt = jnp.pad(wkv[:, :, QK_NOPE:], ((0, 0), (0, 0), (0, V_SLAB - V_DIM))).reshape(
        KV_LORA, MLA_HEADS * V_SLAB).T.astype(BF16)

    def row(v):
        return jnp.pad(v.astype(F32), (0, D_MODEL - v.shape[0]))[None, :]

    rows = [row(norm_g)]
    rows += [row(conv_a_w[k]) for k in range(CONV_A_WIDTH)]
    rows += [row(ssd_conv_w[k]) for k in range(SSD_CONV_WIDTH)]
    rows += [row(ssd_conv_b), row(ssd_dt_bias), row(ssd_a_log), row(jnp.repeat(ssd_d, SSD_HEAD_DIM)),
             row(ssd_norm_g), row(mla_q_norm_g), row(mla_kv_norm_g), row(final_norm_g)]
    rows += [jnp.zeros((N_VEC_ROWS - len(rows), D_MODEL), F32)]
    vecs = jnp.concatenate(rows, axis=0)
    return vecs, wqbt, wk, wvt


def _vmem_limit(tile, n_tiles):
    f32, bf16 = 4, 2
    blocks = 2 * (2 * tile * D_MODEL * f32)
    weights = 2 * bf16 * (D_MODEL * N_PROJ + Q_LORA * MLA_HEADS * HEAD_SLAB
                          + KV_LORA * MLA_HEADS * (HEAD_SLAB + V_SLAB) + D_MIX * D_MODEL)
    scratch = (tile * N_PROJ * f32 + (tile + HALO) * (D_CONV_A + SSD_CONV_DIM) * f32
               + SSD_STATE * D_SSD * f32 + n_tiles * tile * MLA_HEADS * (HEAD_SLAB + V_SLAB) * bf16
               + MLA_HEADS * HEAD_SLAB * tile * bf16 + MLA_HEADS * V_SLAB * tile * f32
               + D_MLA * tile * f32 + tile * D_MIX * bf16)
    temporaries = 2 * tile * N_PROJ * f32
    return min(blocks + weights + scratch + temporaries, V7X_VMEM_BYTES - 8 * 1024 * 1024)


def _layer_call(x, pos3, invf, params, w_in_all, w_out_all, *, layer, final):
    batch, seq, d_model = x.shape
    tile = SEQ_TILE
    n_tiles = seq // tile
    vecs, wqbt, wk, wvt = params
    const = lambda shape: pl.BlockSpec(shape, lambda b, j: (0,) * len(shape))
    of_layer = lambda a: pl.BlockSpec((None,) + a.shape[1:], lambda b, j: (layer, 0, 0))
    return pl.pallas_call(
        functools.partial(_layer_kernel, final=final),
        out_shape=jax.ShapeDtypeStruct(x.shape, F32),
        grid=(batch, n_tiles),
        in_specs=[
            pl.BlockSpec((None, 1, tile), lambda b, j: (b, 0, j)),
            pl.BlockSpec((None, tile, d_model), lambda b, j: (b, j, 0)),
            const(invf.shape), const(vecs.shape), of_layer(w_in_all),
            const(wqbt.shape), const(wk.shape), const(wvt.shape), of_layer(w_out_all),
        ],
        out_specs=pl.BlockSpec((None, tile, d_model), lambda b, j: (b, j, 0)),
        scratch_shapes=[
            pltpu.VMEM((tile + HALO, D_CONV_A), F32),
            pltpu.VMEM((tile + HALO, SSD_CONV_DIM), F32),
            pltpu.VMEM((SSD_STATE, D_SSD), F32),
            pltpu.VMEM((n_tiles, tile, MLA_HEADS * HEAD_SLAB), BF16),
            pltpu.VMEM((n_tiles, MLA_HEADS * V_SLAB, tile), BF16),
            pltpu.VMEM((MLA_HEADS * HEAD_SLAB, tile), BF16),
            pltpu.VMEM((MLA_HEADS * V_SLAB, tile), F32),
            pltpu.VMEM((MLA_HEADS, 1, tile), F32),
            pltpu.VMEM((MLA_HEADS * V_SLAB, tile), F32),
            pltpu.VMEM((MLA_HEADS, 1, tile), F32),
            pltpu.VMEM((D_MLA, tile), F32),
            pltpu.VMEM((tile, D_MIX), BF16),
            pltpu.VMEM((SSD_CHUNK, SSD_CHUNK), BF16),
            pltpu.VMEM((LANES, D_SSD), BF16),
        ] + [pltpu.VMEM((SSD_CHUNK, N_PROJ), F32)] * (tile // SSD_CHUNK),
        compiler_params=pltpu.CompilerParams(
            dimension_semantics=("arbitrary", "arbitrary"),
            vmem_limit_bytes=_vmem_limit(tile, n_tiles)),
        name="hybrid_layer_final" if final else "hybrid_layer",
    )(pos3, x, invf, vecs, w_in_all, wqbt, wk, wvt, w_out_all)


def kernel(x, positions, norm_g, w_in, conv_a_w, ssd_conv_w, ssd_conv_b, ssd_dt_bias, ssd_a_log, ssd_d,
           ssd_norm_g, mla_q_norm_g, w_qb, mla_kv_norm_g, w_kvb, w_out, final_norm_g):
    batch, seq, _ = x.shape
    depth = norm_g.shape[0]
    assert seq % SEQ_TILE == 0 and SEQ_TILE % SSD_CHUNK == 0 and SEQ_TILE % KEY_BLOCK == 0
    inv_freq = ROPE_BASE ** (-jnp.arange(0, QK_ROPE, 2, dtype=F32) / QK_ROPE)
    invf = jnp.broadcast_to(inv_freq[:, None], (ROPE_HALF, SEQ_TILE))
    pos3 = positions.reshape(batch, 1, seq)
    w_in_all = jnp.pad(w_in.astype(BF16), ((0, 0), (0, 0), (0, N_PROJ - IN_COLS)))
    w_out_all = w_out.astype(BF16)
    for l in range(depth):
        params = _prep_layer(norm_g[l], conv_a_w[l], ssd_conv_w[l], ssd_conv_b[l], ssd_dt_bias[l],
                             ssd_a_log[l], ssd_d[l], ssd_norm_g[l], mla_q_norm_g[l], w_qb[l],
                             mla_kv_norm_g[l], w_kvb[l], final_norm_g)
        x = _layer_call(x, pos3, invf, params, w_in_all, w_out_all, layer=l, final=(l == depth - 1))
    return x
```
